```python
import math
import jax, jax.numpy as jnp
from jax import lax
import numpy as np

D_MODEL = 1024
BATCH = 16
SEQ = 2048
DEPTH = 2
DEC_BATCH = 4
DEC_SEQ = 8192
PAST_LEN = 128

N_META = 16
N_EVEN = (DEPTH + 1) // 2
N_ODD = DEPTH // 2
D_A = D_MODEL // 2
S5_GROUP = 16
G_A = D_A // S5_GROUP
S5_STATE = 64
D_B = D_MODEL // 2
H_B = 8
BW_B = D_B // H_B
LRU_C = 8.0
CONV_B = 4
CONV_B_LEFT = 2
D_IN_AB = D_A + 2 * D_B
D_MIX_AB = D_A + D_B
HEAD_DIM = 64
N_Q_HEADS = D_MODEL // HEAD_DIM
N_KV_HEADS = 4
GQ = N_Q_HEADS // N_KV_HEADS
D_QKV = (N_Q_HEADS + 2 * N_KV_HEADS) * HEAD_DIM
WINDOW = 128
BLOCK = 128
D_FF = 2816
CONV_F = 3
EPS = 1e-6
NEG = -1e30

kernel_name = 'hybrid_s5_rglru_swa_encoder'


def rmsnorm(x, g):
    xf = x.astype(jnp.float32)
    y = xf * lax.rsqrt(jnp.mean(xf * xf, axis=-1, keepdims=True) + EPS) * g.astype(jnp.float32)
    return y.astype(x.dtype)


def dwconv(x, w, b, left):
    k_width = w.shape[0]
    length = x.shape[1]
    xp = jnp.pad(x, ((0, 0), (left, k_width - 1 - left), (0, 0)))
    out = xp[:, 0:length] * w[0]
    for k in range(1, k_width):
        out = out + xp[:, k:k + length] * w[k]
    return out + b


def _linear_combine(e1, e2):
    a1, b1 = e1
    a2, b2 = e2
    return (a1 * a2, a2 * b1 + b2)


def s5_mixer(u, lam_re, lam_im, log_dt, b_re, b_im, c_re, c_im, d_skip, w_glu, b_glu):
    bsz, length, _ = u.shape
    uf = u.astype(jnp.float32).reshape(bsz, length, G_A, S5_GROUP)
    uc = uf.astype(jnp.complex64)
    y = uf * d_skip.astype(jnp.float32).reshape(G_A, S5_GROUP)
    for direction in range(2):
        lam = lax.complex(jnp.minimum(lam_re[direction].astype(jnp.float32), -1e-4),
                          lam_im[direction].astype(jnp.float32))
        dt = jnp.exp(log_dt[direction].astype(jnp.float32))[:, None]
        lam_bar = jnp.exp(lam * dt)
        b_mat = lax.complex(b_re[direction].astype(jnp.float32), b_im[direction].astype(jnp.float32))
        b_bar = ((lam_bar - 1.0) / lam)[:, :, None] * b_mat
        bu = jnp.einsum('blgc,gnc->blgn', uc, b_bar)
        a = jnp.broadcast_to(lam_bar, bu.shape)
        _, states = lax.associative_scan(_linear_combine, (a, bu), axis=1, reverse=(direction == 1))
        c_mat = lax.complex(c_re[direction].astype(jnp.float32), c_im[direction].astype(jnp.float32))
        y = y + jnp.real(jnp.einsum('blgn,gcn->blgc', states, c_mat))
    y = jax.nn.gelu(y.reshape(bsz, length, D_A))
    y = y * jax.nn.sigmoid(y @ w_glu.astype(jnp.float32) + b_glu.astype(jnp.float32))
    return y.astype(u.dtype)


def rglru_direction(x, w_r, b_r, w_i, b_i, lam, reverse):
    bsz, length, _ = x.shape
    xh = x.reshape(bsz, length, H_B, BW_B)
    r = jax.nn.sigmoid(jnp.einsum('blhi,hij->blhj', xh, w_r.astype(jnp.float32)).reshape(bsz, length, D_B) + b_r.astype(jnp.float32))
    i = jax.nn.sigmoid(jnp.einsum('blhi,hij->blhj', xh, w_i.astype(jnp.float32)).reshape(bsz, length, D_B) + b_i.astype(jnp.float32))
    log_a = -LRU_C * r * jax.nn.softplus(-lam.astype(jnp.float32))
    a = jnp.exp(log_a)
    inp = jnp.sqrt(-jnp.expm1(2.0 * log_a)) * (i * x)
    _, h = lax.associative_scan(_linear_combine, (a, inp), axis=1, reverse=reverse)
    return h


def rglru_mixer(xb, gate, conv_w, conv_b, w_r, b_r, w_i, b_i, lam):
    xc = dwconv(xb, conv_w, conv_b, CONV_B_LEFT).astype(jnp.float32)
    h = (rglru_direction(xc, w_r[0], b_r[0], w_i[0], b_i[0], lam[0], False)
         + rglru_direction(xc, w_r[1], b_r[1], w_i[1], b_i[1], lam[1], True))
    return (h * jax.nn.gelu(gate.astype(jnp.float32))).astype(xb.dtype)


def ab_layer(h, w_in, lam_re, lam_im, log_dt, b_re, b_im, c_re, c_im, d_skip, w_glu, b_glu,
             conv_w, conv_b, w_r, b_r, w_i, b_i, lam, w_out):
    z = h @ w_in
    u_a = z[..., :D_A]
    x_b = z[..., D_A:D_A + D_B]
    g_b = z[..., D_A + D_B:]
    y_a = s5_mixer(u_a, lam_re, lam_im, log_dt, b_re, b_im, c_re, c_im, d_skip, w_glu, b_glu)
    y_b = rglru_mixer(x_b, g_b, conv_w, conv_b, w_r, b_r, w_i, b_i, lam)
    return jnp.concatenate([y_a, y_b], axis=-1) @ w_out


def alibi_slopes():
    return 2.0 ** (-8.0 * jnp.arange(1, N_Q_HEADS + 1, dtype=jnp.float32) / N_Q_HEADS)


def windowed_gqa(h, w_qkv, w_o, sink):
    bsz, length, _ = h.shape
    qkv = h @ w_qkv
    q = qkv[..., :N_Q_HEADS * HEAD_DIM].reshape(bsz, length, N_KV_HEADS, GQ, HEAD_DIM)
    k = qkv[..., N_Q_HEADS * HEAD_DIM:(N_Q_HEADS + N_KV_HEADS) * HEAD_DIM].reshape(bsz, length, N_KV_HEADS, HEAD_DIM)
    v = qkv[..., (N_Q_HEADS + N_KV_HEADS) * HEAD_DIM:].reshape(bsz, length, N_KV_HEADS, HEAD_DIM)
    front = BLOCK - N_META
    lp = length + front
    nb = lp // BLOCK
    qp = jnp.pad(q, ((0, 0), (front, 0), (0, 0), (0, 0), (0, 0))).reshape(bsz, nb, BLOCK, N_KV_HEADS, GQ, HEAD_DIM)
    kp = jnp.pad(k, ((0, 0), (front + BLOCK, BLOCK), (0, 0), (0, 0))).reshape(bsz, nb + 2, BLOCK, N_KV_HEADS, HEAD_DIM)
    vp = jnp.pad(v, ((0, 0), (front + BLOCK, BLOCK), (0, 0), (0, 0))).reshape(bsz, nb + 2, BLOCK, N_KV_HEADS, HEAD_DIM)
    kb = jnp.concatenate([kp[:, :-2], kp[:, 1:-1], kp[:, 2:]], axis=2)
    vb = jnp.concatenate([vp[:, :-2], vp[:, 1:-1], vp[:, 2:]], axis=2)
    s = jnp.einsum('bnqhgd,bnkhd->bnhgqk', qp, kb).astype(jnp.float32) * (1.0 / math.sqrt(HEAD_DIM))
    qi = jnp.arange(BLOCK)
    ki = jnp.arange(3 * BLOCK)
    dist = jnp.abs(qi[:, None] + BLOCK - ki[None, :])
    key_pos = (jnp.arange(nb)[:, None] - 1) * BLOCK + ki[None, :]
    key_ok = (key_pos >= front) & (key_pos < lp)
    mask = (dist <= WINDOW)[None] & key_ok[:, None, :]
    slopes = alibi_slopes().reshape(N_KV_HEADS, GQ)
    s = s - slopes[:, :, None, None] * dist.astype(jnp.float32)
    s = jnp.where(mask[None, :, None, None], s, NEG)
    sink_l = sink.astype(jnp.float32).reshape(N_KV_HEADS, GQ)[None, None, :, :, None]
    m = jnp.maximum(jnp.max(s, axis=-1), sink_l)
    p = jnp.exp(s - m[..., None])
    denom = jnp.sum(p, axis=-1) + jnp.exp(sink_l - m)
    o = jnp.einsum('bnhgqk,bnkhd->bnqhgd', p.astype(vb.dtype), vb)
    o = o / jnp.moveaxis(denom, -1, 2)[..., None].astype(o.dtype)
    o = o.reshape(bsz, lp, N_Q_HEADS * HEAD_DIM)[:, front:]
    return o @ w_o


def conv_ffn(h, w_up, conv_w, conv_b, w_down):
    u = dwconv(h @ w_up, conv_w, conv_b, CONV_F // 2)
    a = u[..., :D_FF]
    g = u[..., D_FF:]
    return (jax.nn.gelu(g) * a) @ w_down


def trunk(x, meta_tokens, norm_mix_g, norm_ffn_g, final_norm_g, w_in_ab, s5_lambda_re, s5_lambda_im,
          s5_log_dt, s5_b_re, s5_b_im, s5_c_re, s5_c_im, s5_d, w_glu, b_glu, lru_conv_w, lru_conv_b,
          lru_w_r, lru_b_r, lru_w_i, lru_b_i, lru_lambda, w_out_ab, w_qkv, w_o, attn_sink,
          w_up, ffn_conv_w, ffn_conv_b, w_down):
    bsz = x.shape[0]
    meta = jnp.broadcast_to(meta_tokens[None].astype(x.dtype), (bsz, N_META, D_MODEL))
    h = jnp.concatenate([meta, x], axis=1)
    for layer in range(DEPTH):
        j = layer // 2
        hn = rmsnorm(h, norm_mix_g[layer])
        if layer % 2 == 0:
            h = h + ab_layer(hn, w_in_ab[j], s5_lambda_re[j], s5_lambda_im[j], s5_log_dt[j], s5_b_re[j],
                             s5_b_im[j], s5_c_re[j], s5_c_im[j], s5_d[j], w_glu[j], b_glu[j],
                             lru_conv_w[j], lru_conv_b[j], lru_w_r[j], lru_b_r[j], lru_w_i[j], lru_b_i[j],
                             lru_lambda[j], w_out_ab[j])
        else:
            h = h + windowed_gqa(hn, w_qkv[j], w_o[j], attn_sink[j])
        h = h + conv_ffn(rmsnorm(h, norm_ffn_g[layer]), w_up[layer], ffn_conv_w[layer], ffn_conv_b[layer], w_down[layer])
    h = rmsnorm(h, final_norm_g)
    return h[:, N_META:]


def setup_inputs(seed: int = 0) -> dict:
    key = jax.random.key(seed)
    ks = iter(jax.random.split(key, 40))
    f32 = jnp.float32

    def nrm(shape, scale):
        return jax.random.normal(next(ks), shape, f32) * scale

    x_prompt = nrm((BATCH, SEQ, D_MODEL), 1.0)
    x_sample = nrm((DEC_BATCH, DEC_SEQ, D_MODEL), 1.0)
    meta_tokens = nrm((N_META, D_MODEL), 1.0)
    norm_mix_g = 1.0 + nrm((DEPTH, D_MODEL), 0.02)
    norm_ffn_g = 1.0 + nrm((DEPTH, D_MODEL), 0.02)
    final_norm_g = 1.0 + nrm((D_MODEL,), 0.02)
    w_in_ab = nrm((N_EVEN, D_MODEL, D_IN_AB), D_MODEL ** -0.5)
    s5_lambda_re = -0.5 + nrm((N_EVEN, 2, G_A, S5_STATE), 0.01)
    s5_lambda_im = math.pi * jnp.arange(S5_STATE, dtype=f32) + nrm((N_EVEN, 2, G_A, S5_STATE), 0.01)
    s5_log_dt = jax.random.uniform(next(ks), (N_EVEN, 2, G_A), f32, math.log(1e-3), math.log(1e-1))
    s5_b_re = nrm((N_EVEN, 2, G_A, S5_STATE, S5_GROUP), (2 * S5_GROUP) ** -0.5)
    s5_b_im = nrm((N_EVEN, 2, G_A, S5_STATE, S5_GROUP), (2 * S5_GROUP) ** -0.5)
    s5_c_re = nrm((N_EVEN, 2, G_A, S5_GROUP, S5_STATE), S5_STATE ** -0.5)
    s5_c_im = nrm((N_EVEN, 2, G_A, S5_GROUP, S5_STATE), S5_STATE ** -0.5)
    s5_d = nrm((N_EVEN, D_A), 1.0)
    w_glu = nrm((N_EVEN, D_A, D_A), D_A ** -0.5)
    b_glu = nrm((N_EVEN, D_A), 0.01)
    lru_conv_w = nrm((N_EVEN, CONV_B, D_B), CONV_B ** -0.5)
    lru_conv_b = nrm((N_EVEN, D_B), 0.01)
    lru_w_r = nrm((N_EVEN, 2, H_B, BW_B, BW_B), BW_B ** -0.5)
    lru_b_r = nrm((N_EVEN, 2, D_B), 0.01)
    lru_w_i = nrm((N_EVEN, 2, H_B, BW_B, BW_B), BW_B ** -0.5)
    lru_b_i = nrm((N_EVEN, 2, D_B), 0.01)
    a_c = jax.random.uniform(next(ks), (N_EVEN, 2, D_B), f32, 0.9, 0.999)
    sig = a_c ** (1.0 / LRU_C)
    lru_lambda = jnp.log(sig) - jnp.log1p(-sig)
    w_out_ab = nrm((N_EVEN, D_MIX_AB, D_MODEL), D_MIX_AB ** -0.5)
    w_qkv = nrm((N_ODD, D_MODEL, D_QKV), D_MODEL ** -0.5)
    w_o = nrm((N_ODD, N_Q_HEADS * HEAD_DIM, D_MODEL), (N_Q_HEADS * HEAD_DIM) ** -0.5)
    attn_sink = nrm((N_ODD, N_Q_HEADS), 0.5)
    w_up = nrm((DEPTH, D_MODEL, 2 * D_FF), D_MODEL ** -0.5)
    ffn_conv_w = nrm((DEPTH, CONV_F, 2 * D_FF), CONV_F ** -0.5)
    ffn_conv_b = nrm((DEPTH, 2 * D_FF), 0.01)
    w_down = nrm((DEPTH, D_FF, D_MODEL), D_FF ** -0.5)
    return {'x_prompt': x_prompt, 'x_sample': x_sample, 'meta_tokens': meta_tokens,
            'norm_mix_g': norm_mix_g, 'norm_ffn_g': norm_ffn_g, 'final_norm_g': final_norm_g,
            'w_in_ab': w_in_ab, 's5_lambda_re': s5_lambda_re, 's5_lambda_im': s5_lambda_im,
            's5_log_dt': s5_log_dt, 's5_b_re': s5_b_re, 's5_b_im': s5_b_im, 's5_c_re': s5_c_re,
            's5_c_im': s5_c_im, 's5_d': s5_d, 'w_glu': w_glu, 'b_glu': b_glu,
            'lru_conv_w': lru_conv_w, 'lru_conv_b': lru_conv_b, 'lru_w_r': lru_w_r, 'lru_b_r': lru_b_r,
            'lru_w_i': lru_w_i, 'lru_b_i': lru_b_i, 'lru_lambda': lru_lambda, 'w_out_ab': w_out_ab,
            'w_qkv': w_qkv, 'w_o': w_o, 'attn_sink': attn_sink, 'w_up': w_up,
            'ffn_conv_w': ffn_conv_w, 'ffn_conv_b': ffn_conv_b, 'w_down': w_down}


def reference(x_prompt, x_sample, meta_tokens, norm_mix_g, norm_ffn_g, final_norm_g, w_in_ab,
              s5_lambda_re, s5_lambda_im, s5_log_dt, s5_b_re, s5_b_im, s5_c_re, s5_c_im, s5_d,
              w_glu, b_glu, lru_conv_w, lru_conv_b, lru_w_r, lru_b_r, lru_w_i, lru_b_i, lru_lambda,
              w_out_ab, w_qkv, w_o, attn_sink, w_up, ffn_conv_w, ffn_conv_b, w_down):
    params = (meta_tokens, norm_mix_g, norm_ffn_g, final_norm_g, w_in_ab, s5_lambda_re, s5_lambda_im,
              s5_log_dt, s5_b_re, s5_b_im, s5_c_re, s5_c_im, s5_d, w_glu, b_glu, lru_conv_w, lru_conv_b,
              lru_w_r, lru_b_r, lru_w_i, lru_b_i, lru_lambda, w_out_ab, w_qkv, w_o, attn_sink,
              w_up, ffn_conv_w, ffn_conv_b, w_down)
    y_prompt = trunk(x_prompt, *params)
    y_sample = trunk(x_sample, *params)
    return (y_prompt, y_sample)
```

```python
import functools
import math

import jax
import jax.numpy as jnp
from jax import lax
from jax.experimental import pallas as pl
from jax.experimental.pallas import tpu as pltpu

D_MODEL = 1024
N_META = 16
D_A = 512
S5_GROUP = 16
G_A = D_A // S5_GROUP
S5_STATE = 64
D_B = 512
H_B = 8
BW_B = D_B // H_B
LRU_C = 8.0
CONV_B = 4
CONV_B_LEFT = 2
HEAD_DIM = 64
N_Q_HEADS = 16
N_KV_HEADS = 4
GQ = N_Q_HEADS // N_KV_HEADS
WINDOW = 128
BLOCK = 128
D_FF = 2816
CONV_F = 3
EPS = 1e-6
NEG = -1e30

PAD = BLOCK - N_META
FRONT = PAD + N_META
S5_T = 16
S5_W = S5_T * S5_GROUP
LANES = 128
SUBLANES = 8
BF16_ROWS = 16
FFN_HALO = BF16_ROWS
FFN_TF = 256
LRU_CB = LANES
VMEM_LIMIT = 56 * 1024 * 1024

F32 = jnp.float32
BF16 = jnp.bfloat16


def _cparams(n_axes):
    return pltpu.CompilerParams(
        dimension_semantics=("arbitrary",) * n_axes, vmem_limit_bytes=VMEM_LIMIT)


def _row_tile(rows, lp, cap=512):
    best = BF16_ROWS
    t = BF16_ROWS
    while t <= min(cap, lp):
        if rows % t == 0:
            best = t
        t += BF16_ROWS
    return best


def _lru_tile(lp):
    q = lp // 32
    assert lp % 32 == 0
    odd = q
    while odd % 2 == 0:
        odd //= 2
    best = 1
    for d in range(1, odd + 1, 2):
        if odd % d == 0 and 32 * d <= 640:
            best = d
    return 32 * best


def _rmsnorm(x, g):
    ms = jnp.mean(x * x, axis=-1, keepdims=True)
    return x * lax.rsqrt(ms + EPS) * g


def _gelu(x):
    return x * (0.5 * (1.0 + jnp.tanh(math.sqrt(2.0 / math.pi) * (x + 0.044715 * (x * x * x)))))


def _sigmoid(x):
    return 1.0 / (1.0 + jnp.exp(-x))


def _valid_rows(tile_idx, tm, lp, shape):
    pos = lax.rem(tile_idx * tm, lp) + lax.broadcasted_iota(jnp.int32, shape, 0)
    pos = jnp.where(pos >= lp, pos - lp, pos)
    return pos >= PAD


def _norm_mm_kernel(h_ref, g_ref, w_ref, *o_refs):
    hn = _rmsnorm(h_ref[...], g_ref[...]).astype(BF16)
    off = 0
    for o_ref in o_refs:
        n = o_ref.shape[-1]
        o_ref[...] = jnp.dot(hn, w_ref[:, off:off + n], preferred_element_type=F32).astype(o_ref.dtype)
        off += n


def _norm_mm(h2, g, w, splits, lp):
    rows, d = h2.shape
    tm = _row_tile(rows, lp)
    n = w.shape[1]
    return pl.pallas_call(
        _norm_mm_kernel,
        grid=(rows // tm,),
        in_specs=[pl.BlockSpec((tm, d), lambda i: (i, 0)),
                  pl.BlockSpec((1, d), lambda i: (0, 0)),
                  pl.BlockSpec((d, n), lambda i: (0, 0))],
        out_specs=[pl.BlockSpec((tm, s), lambda i: (i, 0)) for s in splits],
        out_shape=[jax.ShapeDtypeStruct((rows, s), BF16) for s in splits],
        compiler_params=_cparams(1),
        name="norm_mm",
    )(h2, g.reshape(1, d), w)


def _cmul(ar, ai, br, bi):
    return ar * br - ai * bi, ar * bi + ai * br


def _s5_weights(lam_re, lam_im, log_dt, b_re, b_im, c_re, c_im, d_skip):
    hp = lax.Precision.HIGHEST
    t = S5_T
    lr = jnp.minimum(lam_re.astype(F32), -1e-4)
    li = lam_im.astype(F32)
    dt = jnp.exp(log_dt.astype(F32))[..., None]
    mag = jnp.exp(lr * dt)
    lbr, lbi = mag * jnp.cos(li * dt), mag * jnp.sin(li * dt)
    den = lr * lr + li * li
    xr, xi = lbr - 1.0, lbi
    fr, fi = (xr * lr + xi * li) / den, (xi * lr - xr * li) / den
    bbr, bbi = _cmul(fr[..., None], fi[..., None], b_re.astype(F32), b_im.astype(F32))
    pr, pi = [jnp.ones_like(lbr)], [jnp.zeros_like(lbi)]
    for _ in range(t):
        nr, ni = _cmul(pr[-1], pi[-1], lbr, lbi)
        pr.append(nr)
        pi.append(ni)
    pwr, pwi = jnp.stack(pr, -1), jnp.stack(pi, -1)
    cr, ci = c_re.astype(F32), c_im.astype(F32)
    wr, wi = _cmul(cr[..., None], ci[..., None], pwr[:, :, None], pwi[:, :, None])
    kk = (jnp.einsum('dgonk,dgni->dgkio', wr, bbr, precision=hp)
          - jnp.einsum('dgonk,dgni->dgkio', wi, bbi, precision=hp))[:, :, :t]
    eye = jnp.eye(S5_GROUP, dtype=F32) * d_skip.astype(F32).reshape(G_A, 1, S5_GROUP)
    k_all = jnp.concatenate([kk[1][:, 1:][:, ::-1],
                             (kk[0][:, 0] + kk[1][:, 0] + eye)[:, None],
                             kk[0][:, 1:]], axis=1)
    sidx = jnp.arange(t)
    lag = sidx[None, :] - sidx[:, None] + (t - 1)
    m = k_all[:, lag]
    m = m.transpose(0, 1, 3, 2, 4).reshape(G_A, S5_W, S5_W)
    pfr, pfi = _cmul(pwr[0][..., :t][..., ::-1][..., None], pwi[0][..., :t][..., ::-1][..., None],
                     bbr[0][:, :, None, :], bbi[0][:, :, None, :])
    pbr, pbi = _cmul(pwr[1][..., :t][..., None], pwi[1][..., :t][..., None],
                     bbr[1][:, :, None, :], bbi[1][:, :, None, :])

    def p_cols(x):
        x = x.transpose(0, 2, 3, 1).reshape(G_A, S5_W, S5_STATE)
        return jnp.pad(x, ((0, 0), (0, 0), (0, LANES - S5_STATE)))
    p = jnp.concatenate([p_cols(pfr), p_cols(pfi), p_cols(pbr), p_cols(pbi)], axis=-1)

    def q_rows(x):
        x = x.transpose(0, 2, 3, 1).reshape(G_A, S5_STATE, S5_W)
        return jnp.pad(x, ((0, 0), (0, LANES - S5_STATE), (0, 0)))
    q = jnp.concatenate([q_rows(wr[0][..., 1:]), q_rows(-wi[0][..., 1:]),
                         q_rows(wr[1][..., 1:][..., ::-1]), q_rows(-wi[1][..., 1:][..., ::-1])], axis=1)

    def c_row(x):
        return jnp.pad(x, ((0, 0), (0, LANES - S5_STATE)))[:, None]
    coef = jnp.concatenate([c_row(pwr[0][..., t]), c_row(pwi[0][..., t]),
                            c_row(pwr[1][..., t]), c_row(pwi[1][..., t]),
                            jnp.zeros((G_A, 4, LANES), F32)], axis=1)
    return m.astype(BF16), p.astype(BF16), q.astype(BF16), coef


def _s5_kernel(u_ref, m_ref, p_ref, q_ref, coef_ref, y_ref, s_scr, x0_scr, *, n_chunks, bp, n_blk):
    rows = n_chunks * bp
    rb = rows // n_blk
    for k in range(n_blk):
        sl = slice(k * rb, (k + 1) * rb)
        s_scr[sl, :] = jnp.dot(u_ref[sl, :], p_ref[...], preferred_element_type=F32)

    shp = (bp, LANES)
    lfr = jnp.broadcast_to(coef_ref[0:1, :], shp)
    lfi = jnp.broadcast_to(coef_ref[1:2, :], shp)
    lbr = jnp.broadcast_to(coef_ref[2:3, :], shp)
    lbi = jnp.broadcast_to(coef_ref[3:4, :], shp)

    def step(j, carry):
        xfr, xfi, xbr, xbi = carry
        rf = pl.ds(pl.multiple_of(j * bp, SUBLANES), bp)
        rbk = pl.ds(pl.multiple_of((n_chunks - 1 - j) * bp, SUBLANES), bp)
        x0_scr[rf, 0:LANES] = xfr
        x0_scr[rf, LANES:2 * LANES] = xfi
        x0_scr[rbk, 2 * LANES:3 * LANES] = xbr
        x0_scr[rbk, 3 * LANES:4 * LANES] = xbi
        nfr = lfr * xfr - lfi * xfi + s_scr[rf, 0:LANES]
        nfi = lfr * xfi + lfi * xfr + s_scr[rf, LANES:2 * LANES]
        nbr = lbr * xbr - lbi * xbi + s_scr[rbk, 2 * LANES:3 * LANES]
        nbi = lbr * xbi + lbi * xbr + s_scr[rbk, 3 * LANES:4 * LANES]
        return nfr, nfi, nbr, nbi

    z = jnp.zeros(shp, F32)
    lax.fori_loop(0, n_chunks, step, (z, z, z, z))

    for k in range(n_blk):
        sl = slice(k * rb, (k + 1) * rb)
        y = jnp.dot(u_ref[sl, :], m_ref[...], preferred_element_type=F32)
        y = y + jnp.dot(x0_scr[sl, :].astype(BF16), q_ref[...], preferred_element_type=F32)
        y_ref[sl, :] = y.astype(y_ref.dtype)


def _s5_mixer(u_a, weights, bsz, lp):
    m, p, q, coef = weights
    n_chunks = lp // S5_T
    bp = -(-bsz // SUBLANES) * SUBLANES
    rows = n_chunks * bp
    n_blk = 4 if rows % (4 * BF16_ROWS) == 0 else 1
    ut = u_a.reshape(bsz, n_chunks, S5_T, G_A, S5_GROUP).transpose(3, 1, 0, 2, 4)
    ut = jnp.pad(ut, ((0, 0), (0, 0), (0, bp - bsz), (0, 0), (0, 0))).reshape(G_A, rows, S5_W)
    yt = pl.pallas_call(
        functools.partial(_s5_kernel, n_chunks=n_chunks, bp=bp, n_blk=n_blk),
        grid=(G_A,),
        in_specs=[pl.BlockSpec((None, rows, S5_W), lambda g: (g, 0, 0)),
                  pl.BlockSpec((None, S5_W, S5_W), lambda g: (g, 0, 0)),
                  pl.BlockSpec((None, S5_W, 4 * LANES), lambda g: (g, 0, 0)),
                  pl.BlockSpec((None, 4 * LANES, S5_W), lambda g: (g, 0, 0)),
                  pl.BlockSpec((None, SUBLANES, LANES), lambda g: (g, 0, 0))],
        out_specs=pl.BlockSpec((None, rows, S5_W), lambda g: (g, 0, 0)),
        out_shape=jax.ShapeDtypeStruct((G_A, rows, S5_W), BF16),
        scratch_shapes=[pltpu.VMEM((rows, 4 * LANES), F32), pltpu.VMEM((rows, 4 * LANES), F32)],
        compiler_params=_cparams(1),
        name="s5_chunks",
    )(ut, m, p, q, coef)
    yt = yt.reshape(G_A, n_chunks, bp, S5_T, S5_GROUP)[:, :, :bsz]
    return yt.transpose(2, 1, 3, 0, 4).reshape(bsz * lp, D_A)


def _lru_weights(conv_w, conv_b, w_r, b_r, w_i, b_i, lam):
    ncb = D_B // LRU_CB
    hpb = LRU_CB // BW_B

    def blockdiag(w):
        w = w.astype(F32).reshape(2, ncb, hpb, BW_B, BW_B)
        out = jnp.zeros((2, ncb, hpb, BW_B, hpb, BW_B), F32)
        for k in range(hpb):
            out = out.at[:, :, k, :, k, :].set(w[:, :, k])
        return out.reshape(2, ncb, LRU_CB, LRU_CB)
    wg = jnp.concatenate([blockdiag(w_r), blockdiag(w_i)], axis=-1).astype(BF16)
    bg = jnp.concatenate([b_r.astype(F32).reshape(2, ncb, 1, LRU_CB),
                          b_i.astype(F32).reshape(2, ncb, 1, LRU_CB)], axis=-1)
    cl = (-LRU_C * jax.nn.softplus(-lam.astype(F32))).reshape(2, ncb, 1, LRU_CB)
    cw = jnp.pad(conv_w.astype(F32), ((0, SUBLANES - CONV_B), (0, 0)))
    cw = cw.reshape(SUBLANES, ncb, LRU_CB).transpose(1, 0, 2)
    cb = conv_b.astype(F32).reshape(ncb, 1, LRU_CB)
    return cw, cb, wg, bg, cl


def _lru_kernel(x_ref, g_ref, cw_ref, cb_ref, wg_ref, bg_ref, cl_ref, o_ref,
                xext, a_scr, b_scr, hf_scr, hb_scr, init_scr, *, lp, tt):
    seg = tt // SUBLANES
    n_tiles = lp // tt
    sb = tt // 4
    halo = SUBLANES
    zrow = jnp.zeros((halo, LANES), F32)
    xext[0:halo, :] = zrow
    xext[halo:halo + lp, :] = x_ref[...].astype(F32)
    xext[halo + lp:2 * halo + lp, :] = zrow

    def gates(d, t0):
        n = sb + 2 * halo
        for k in range(tt // sb):
            base = pl.multiple_of(t0 + k * sb, SUBLANES)
            win = xext[pl.ds(base, n), :]
            xc = (cw_ref[0:1, :] * pltpu.roll(win, 2, 0)[halo:halo + sb]
                  + cw_ref[1:2, :] * pltpu.roll(win, 1, 0)[halo:halo + sb]
                  + cw_ref[2:3, :] * win[halo:halo + sb]
                  + cw_ref[3:4, :] * pltpu.roll(win, n - 1, 0)[halo:halo + sb]) + cb_ref[...]
            z = jnp.dot(xc.astype(BF16), wg_ref[d], preferred_element_type=F32) + bg_ref[d]
            r = _sigmoid(z[:, :LRU_CB])
            ig = _sigmoid(z[:, LRU_CB:])
            a = jnp.exp(cl_ref[d] * r)
            inp = jnp.sqrt(1.0 - a * a) * (ig * xc)
            row = base + lax.broadcasted_iota(jnp.int32, (sb, LANES), 0)
            inp = jnp.where(row >= PAD, inp, 0.0)
            a_scr[d, k * sb:(k + 1) * sb, :] = a
            b_scr[d, k * sb:(k + 1) * sb, :] = inp

    a_f, b_f, a_b, b_b = a_scr.at[0], b_scr.at[0], a_scr.at[1], b_scr.at[1]

    def tile(jt, carry):
        hf_in, hb_in = carry
        t0f = jt * tt
        t0b = (n_tiles - 1 - jt) * tt
        gates(0, t0f)
        gates(1, t0b)

        def p1(i, c):
            hf, af, hb, ab = c
            rf = pl.ds(i, SUBLANES, stride=seg)
            rb = pl.ds(seg - 1 - i, SUBLANES, stride=seg)
            a1, a2 = a_f[rf, :], a_b[rb, :]
            return a1 * hf + b_f[rf, :], a1 * af, a2 * hb + b_b[rb, :], a2 * ab

        z8 = jnp.zeros((SUBLANES, LANES), F32)
        o8 = jnp.ones((SUBLANES, LANES), F32)
        hfe, afe, hbe, abe = lax.fori_loop(0, seg, p1, (z8, o8, z8, o8))
        c = hf_in
        for s in range(SUBLANES):
            init_scr[0, s:s + 1, :] = c
            c = hfe[s:s + 1] + afe[s:s + 1] * c
        hf_out = c
        c = hb_in
        for s in reversed(range(SUBLANES)):
            init_scr[1, s:s + 1, :] = c
            c = hbe[s:s + 1] + abe[s:s + 1] * c
        hb_out = c

        def p2(i, c):
            hf, hb = c
            rf = pl.ds(i, SUBLANES, stride=seg)
            rb = pl.ds(seg - 1 - i, SUBLANES, stride=seg)
            hf = a_f[rf, :] * hf + b_f[rf, :]
            hb = a_b[rb, :] * hb + b_b[rb, :]
            hf_scr[pl.ds(t0f + i, SUBLANES, stride=seg), :] = hf
            hb_scr[pl.ds(t0b + seg - 1 - i, SUBLANES, stride=seg), :] = hb
            return hf, hb

        lax.fori_loop(0, seg, p2, (init_scr[0], init_scr[1]))
        return hf_out, hb_out

    z1 = jnp.zeros((1, LANES), F32)
    lax.fori_loop(0, n_tiles, tile, (z1, z1))

    cbk = tt // 2

    def combine(k, _):
        rs = pl.ds(pl.multiple_of(k * cbk, BF16_ROWS), cbk)
        h = hf_scr[rs, :] + hb_scr[rs, :]
        o_ref[rs, :] = (h * _gelu(g_ref[rs, :].astype(F32))).astype(o_ref.dtype)
        return 0

    lax.fori_loop(0, lp // cbk, combine, 0)


def _lru_mixer(x_b, g_b, weights, bsz, lp):
    cw, cb, wg, bg, cl = weights
    ncb = D_B // LRU_CB
    tt = _lru_tile(lp)
    x3 = x_b.reshape(bsz, lp, D_B)
    g3 = g_b.reshape(bsz, lp, D_B)
    seq_spec = pl.BlockSpec((None, lp, LRU_CB), lambda b, c: (b, 0, c))
    out = pl.pallas_call(
        functools.partial(_lru_kernel, lp=lp, tt=tt),
        grid=(bsz, ncb),
        in_specs=[seq_spec, seq_spec,
                  pl.BlockSpec((None, SUBLANES, LRU_CB), lambda b, c: (c, 0, 0)),
                  pl.BlockSpec((None, 1, LRU_CB), lambda b, c: (c, 0, 0)),
                  pl.BlockSpec((2, None, LRU_CB, 2 * LRU_CB), lambda b, c: (0, c, 0, 0)),
                  pl.BlockSpec((2, None, 1, 2 * LRU_CB), lambda b, c: (0, c, 0, 0)),
                  pl.BlockSpec((2, None, 1, LRU_CB), lambda b, c: (0, c, 0, 0))],
        out_specs=seq_spec,
        out_shape=jax.ShapeDtypeStruct((bsz, lp, D_B), BF16),
        scratch_shapes=[pltpu.VMEM((lp + 2 * SUBLANES, LANES), F32),
                        pltpu.VMEM((2, tt, LANES), F32), pltpu.VMEM((2, tt, LANES), F32),
                        pltpu.VMEM((lp, LANES), F32), pltpu.VMEM((lp, LANES), F32),
                        pltpu.VMEM((2, SUBLANES, LANES), F32)],
        compiler_params=_cparams(2),
        name="rglru",
    )(x3, g3, cw, cb, wg, bg, cl)
    return out.reshape(bsz * lp, D_B)


def _ab_out_kernel(ya_ref, yb_ref, h_ref, wglu_ref, bglu_ref, wo_ref, o_ref, *, lp, tm):
    ya = _gelu(ya_ref[...].astype(F32))
    gate = _sigmoid(jnp.dot(ya.astype(BF16), wglu_ref[...], preferred_element_type=F32) + bglu_ref[...])
    ya = (ya * gate).astype(BF16)
    acc = jnp.dot(ya, wo_ref[0:D_A, :], preferred_element_type=F32)
    acc = acc + jnp.dot(yb_ref[...], wo_ref[D_A:, :], preferred_element_type=F32)
    valid = _valid_rows(pl.program_id(0), tm, lp, acc.shape)
    o_ref[...] = jnp.where(valid, h_ref[...] + acc, 0.0)


def _ab_out(ya, yb, h2, w_glu, b_glu, w_out, lp):
    rows, d = h2.shape
    tm = _row_tile(rows, lp)
    return pl.pallas_call(
        functools.partial(_ab_out_kernel, lp=lp, tm=tm),
        grid=(rows // tm,),
        in_specs=[pl.BlockSpec((tm, D_A), lambda i: (i, 0)),
                  pl.BlockSpec((tm, D_B), lambda i: (i, 0)),
                  pl.BlockSpec((tm, d), lambda i: (i, 0)),
                  pl.BlockSpec((D_A, D_A), lambda i: (0, 0)),
                  pl.BlockSpec((1, D_A), lambda i: (0, 0)),
                  pl.BlockSpec((D_A + D_B, d), lambda i: (0, 0))],
        out_specs=pl.BlockSpec((tm, d), lambda i: (i, 0)),
        out_shape=jax.ShapeDtypeStruct((rows, d), F32),
        compiler_params=_cparams(1),
        name="ab_out",
    )(ya, yb, h2, w_glu.astype(BF16), b_glu.astype(F32).reshape(1, D_A), w_out.astype(BF16))


def _proj_res_kernel(x_ref, h_ref, w_ref, o_ref, *, lp, tm):
    acc = jnp.dot(x_ref[...], w_ref[...], preferred_element_type=F32)
    valid = _valid_rows(pl.program_id(0), tm, lp, acc.shape)
    o_ref[...] = jnp.where(valid, h_ref[...] + acc, 0.0)


def _proj_res(x, h2, w, lp):
    rows, d = h2.shape
    tm = _row_tile(rows, lp)
    k = x.shape[1]
    return pl.pallas_call(
        functools.partial(_proj_res_kernel, lp=lp, tm=tm),
        grid=(rows // tm,),
        in_specs=[pl.BlockSpec((tm, k), lambda i: (i, 0)),
                  pl.BlockSpec((tm, d), lambda i: (i, 0)),
                  pl.BlockSpec((k, d), lambda i: (0, 0))],
        out_specs=pl.BlockSpec((tm, d), lambda i: (i, 0)),
        out_shape=jax.ShapeDtypeStruct((rows, d), F32),
        compiler_params=_cparams(1),
        name="proj_res",
    )(x, h2, w.astype(BF16))


def _attn_kernel(q_ref, kp_ref, kc_ref, kn_ref, vp_ref, vc_ref, vn_ref, sc_ref, o_ref, *, lp):
    n = pl.program_id(1)
    shape = (BLOCK, 3 * BLOCK)
    qi = lax.broadcasted_iota(jnp.int32, shape, 0)
    ki = lax.broadcasted_iota(jnp.int32, shape, 1)
    dist = jnp.abs(qi + BLOCK - ki)
    key_pos = (n - 1) * BLOCK + ki
    ok = (dist <= WINDOW) & (key_pos >= PAD) & (key_pos < lp)
    distf = dist.astype(F32)
    k_all = jnp.concatenate([kp_ref[...], kc_ref[...], kn_ref[...]], axis=0)
    v_all = jnp.concatenate([vp_ref[...], vc_ref[...], vn_ref[...]], axis=0)
    scale = 1.0 / math.sqrt(HEAD_DIM)
    for hk in range(N_KV_HEADS):
        kh = k_all[:, hk * HEAD_DIM:(hk + 1) * HEAD_DIM]
        vh = v_all[:, hk * HEAD_DIM:(hk + 1) * HEAD_DIM]
        for g in range(GQ):
            hq = hk * GQ + g
            qh = q_ref[:, hq * HEAD_DIM:(hq + 1) * HEAD_DIM]
            s = lax.dot_general(qh, kh, (((1,), (1,)), ((), ())), preferred_element_type=F32) * scale
            s = jnp.where(ok, s - sc_ref[0, hq] * distf, NEG)
            sink = sc_ref[1, hq]
            m = jnp.maximum(jnp.max(s, axis=-1, keepdims=True), sink)
            p = jnp.exp(s - m)
            denom = jnp.sum(p, axis=-1, keepdims=True) + jnp.exp(sink - m)
            o = jnp.dot(p.astype(BF16), vh, preferred_element_type=F32) / denom
            o_ref[:, hq * HEAD_DIM:(hq + 1) * HEAD_DIM] = o.astype(o_ref.dtype)


def _attention(q, k, v, sink, bsz, lp):
    nb = lp // BLOCK
    dq = N_Q_HEADS * HEAD_DIM
    dkv = N_KV_HEADS * HEAD_DIM
    q3, k3, v3 = q.reshape(bsz, lp, dq), k.reshape(bsz, lp, dkv), v.reshape(bsz, lp, dkv)
    slopes = 2.0 ** (-8.0 * jnp.arange(1, N_Q_HEADS + 1, dtype=F32) / N_Q_HEADS)
    sc = jnp.stack([slopes, sink.astype(F32)], axis=0)
    kv_prev = pl.BlockSpec((None, BLOCK, dkv), lambda b, n: (b, jnp.maximum(n - 1, 0), 0))
    kv_cur = pl.BlockSpec((None, BLOCK, dkv), lambda b, n: (b, n, 0))
    kv_next = pl.BlockSpec((None, BLOCK, dkv), lambda b, n: (b, jnp.minimum(n + 1, nb - 1), 0))
    out = pl.pallas_call(
        functools.partial(_attn_kernel, lp=lp),
        grid=(bsz, nb),
        in_specs=[pl.BlockSpec((None, BLOCK, dq), lambda b, n: (b, n, 0)),
                  kv_prev, kv_cur, kv_next, kv_prev, kv_cur, kv_next,
                  pl.BlockSpec(memory_space=pltpu.SMEM)],
        out_specs=pl.BlockSpec((None, BLOCK, dq), lambda b, n: (b, n, 0)),
        out_shape=jax.ShapeDtypeStruct((bsz, lp, dq), BF16),
        compiler_params=_cparams(2),
        name="swa",
    )(q3, k3, k3, k3, v3, v3, v3, sc)
    return out.reshape(bsz * lp, dq)


def _ffn_kernel(hp_ref, hc_ref, hn_ref, g_ref, wup_ref, cw_ref, cb_ref, wdn_ref, gf_ref, o_ref,
                hx_scr, act_scr, *, lp, tm, n_tiles, final):
    i = pl.program_id(0)
    g = g_ref[...]
    hc = hc_ref[...]
    n = tm + 2 * FFN_HALO
    hx_scr[0:FFN_HALO, :] = _rmsnorm(hp_ref[...], g).astype(BF16)
    hx_scr[FFN_HALO:FFN_HALO + tm, :] = _rmsnorm(hc, g).astype(BF16)
    keep = jnp.where(i < n_tiles - 1, 1.0, 0.0)
    hx_scr[FFN_HALO + tm:n, :] = (_rmsnorm(hn_ref[...], g) * keep).astype(BF16)

    def conv(u, cols):
        lo, hi = FFN_HALO, FFN_HALO + tm
        return (cw_ref[0:1, cols] * pltpu.roll(u, 1, 0)[lo:hi]
                + cw_ref[1:2, cols] * u[lo:hi]
                + cw_ref[2:3, cols] * pltpu.roll(u, n - 1, 0)[lo:hi]) + cb_ref[0:1, cols]

    for c in range(D_FF // FFN_TF):
        ca = slice(c * FFN_TF, (c + 1) * FFN_TF)
        cg = slice(D_FF + c * FFN_TF, D_FF + (c + 1) * FFN_TF)
        ua = jnp.dot(hx_scr[...], wup_ref[:, ca], preferred_element_type=F32)
        ug = jnp.dot(hx_scr[...], wup_ref[:, cg], preferred_element_type=F32)
        act_scr[:, ca] = (_gelu(conv(ug, cg)) * conv(ua, ca)).astype(BF16)

    y = hc + jnp.dot(act_scr[...], wdn_ref[...], preferred_element_type=F32)
    if final:
        o_ref[...] = _rmsnorm(y, gf_ref[...])
    else:
        o_ref[...] = jnp.where(_valid_rows(i, tm, lp, y.shape), y, 0.0)


def _conv_ffn(h2, g, w_up, conv_w, conv_b, w_down, g_final, lp, final):
    rows, d = h2.shape
    tm = _row_tile(rows, lp)
    n_tiles = rows // tm
    hb = tm // FFN_HALO
    last = rows // FFN_HALO - 1
    const = dict(pipeline_mode=pl.Buffered(1))
    cw = jnp.pad(conv_w.astype(F32), ((0, SUBLANES - CONV_F), (0, 0)))
    return pl.pallas_call(
        functools.partial(_ffn_kernel, lp=lp, tm=tm, n_tiles=n_tiles, final=final),
        grid=(n_tiles,),
        in_specs=[pl.BlockSpec((FFN_HALO, d), lambda i: (jnp.maximum(i * hb - 1, 0), 0)),
                  pl.BlockSpec((tm, d), lambda i: (i, 0)),
                  pl.BlockSpec((FFN_HALO, d), lambda i: (jnp.minimum((i + 1) * hb, last), 0)),
                  pl.BlockSpec((1, d), lambda i: (0, 0)),
                  pl.BlockSpec((d, 2 * D_FF), lambda i: (0, 0), **const),
                  pl.BlockSpec((SUBLANES, 2 * D_FF), lambda i: (0, 0)),
                  pl.BlockSpec((1, 2 * D_FF), lambda i: (0, 0)),
                  pl.BlockSpec((D_FF, d), lambda i: (0, 0), **const),
                  pl.BlockSpec((1, d), lambda i: (0, 0))],
        out_specs=pl.BlockSpec((tm, d), lambda i: (i, 0)),
        out_shape=jax.ShapeDtypeStruct((rows, d), F32),
        scratch_shapes=[pltpu.VMEM((tm + 2 * FFN_HALO, d), BF16), pltpu.VMEM((tm, D_FF), BF16)],
        compiler_params=_cparams(1),
        name="conv_ffn",
    )(h2, h2, h2, g.astype(F32).reshape(1, d), w_up.astype(BF16), cw,
      conv_b.astype(F32).reshape(1, 2 * D_FF), w_down.astype(BF16), g_final.astype(F32).reshape(1, d))


def _trunk(x, prm):
    bsz, seq, d = x.shape
    lp = seq + FRONT
    assert seq % BLOCK == 0 and d == D_MODEL
    meta = jnp.broadcast_to(prm['meta_tokens'][None].astype(F32), (bsz, N_META, d))
    h = jnp.concatenate([jnp.zeros((bsz, PAD, d), F32), meta, x.astype(F32)], axis=1).reshape(bsz * lp, d)

    u_a, x_b, g_b = _norm_mm(h, prm['norm_mix_g'][0], prm['w_in_ab'][0].astype(BF16), (D_A, D_B, D_B), lp)
    s5w = _s5_weights(prm['s5_lambda_re'][0], prm['s5_lambda_im'][0], prm['s5_log_dt'][0],
                      prm['s5_b_re'][0], prm['s5_b_im'][0], prm['s5_c_re'][0], prm['s5_c_im'][0], prm['s5_d'][0])
    y_a = _s5_mixer(u_a, s5w, bsz, lp)
    lruw = _lru_weights(prm['lru_conv_w'][0], prm['lru_conv_b'][0], prm['lru_w_r'][0], prm['lru_b_r'][0],
                        prm['lru_w_i'][0], prm['lru_b_i'][0], prm['lru_lambda'][0])
    y_b = _lru_mixer(x_b, g_b, lruw, bsz, lp)
    h = _ab_out(y_a, y_b, h, prm['w_glu'][0], prm['b_glu'][0], prm['w_out_ab'][0], lp)
    h = _conv_ffn(h, prm['norm_ffn_g'][0], prm['w_up'][0], prm['ffn_conv_w'][0], prm['ffn_conv_b'][0],
                  prm['w_down'][0], prm['final_norm_g'], lp, final=False)

    q, k, v = _norm_mm(h, prm['norm_mix_g'][1], prm['w_qkv'][0].astype(BF16),
                       (N_Q_HEADS * HEAD_DIM, N_KV_HEADS * HEAD_DIM, N_KV_HEADS * HEAD_DIM), lp)
    o = _attention(q, k, v, prm['attn_sink'][0], bsz, lp)
    h = _proj_res(o, h, prm['w_o'][0], lp)
    h = _conv_ffn(h, prm['norm_ffn_g'][1], prm['w_up'][1], prm['ffn_conv_w'][1], prm['ffn_conv_b'][1],
                  prm['w_down'][1], prm['final_norm_g'], lp, final=True)
    return h.reshape(bsz, lp, d)[:, FRONT:]


def kernel(x_prompt, x_sample, meta_tokens, norm_mix_g, norm_ffn_g, final_norm_g, w_in_ab, s5_lambda_re, s5_lambda_im, s5_log_dt, s5_b_re, s5_b_im, s5_c_re, s5_c_im, s5_d, w_glu, b_glu, lru_conv_w, lru_conv_b, lru_w_r, lru_b_r, lru_w_i, lru_b_i, lru_lambda, w_out_ab, w_qkv, w_o, attn_sink, w_up, ffn_conv_w, ffn_conv_b, w_down):
    prm = dict(meta_tokens=meta_tokens, norm_mix_g=norm_mix_g, norm_ffn_g=norm_ffn_g, final_norm_g=final_norm_g,
               w_in_ab=w_in_ab, s5_lambda_re=s5_lambda_re, s5_lambda_im=s5_lambda_im, s5_log_dt=s5_log_dt,
               s5_b_re=s5_b_re, s5_b_im=s5_b_im, s5_c_re=s5_c_re, s5_c_im=s5_c_im, s5_d=s5_d, w_glu=w_glu,
               b_glu=b_glu, lru_conv_w=lru_conv_w, lru_conv_b=lru_conv_b, lru_w_r=lru_w_r, lru_b_r=lru_b_r,
               lru_w_i=lru_w_i, lru_b_i=lru_b_i, lru_lambda=lru_lambda, w_out_ab=w_out_ab, w_qkv=w_qkv,
               w_o=w_o, attn_sink=attn_sink, w_up=w_up, ffn_conv_w=ffn_conv_w, ffn_conv_b=ffn_conv_b,
               w_down=w_down)
    return (_trunk(x_prompt, prm), _trunk(x_sample, prm))
```

```python
import functools
import math

import jax
import jax.numpy as jnp
from jax import lax
from jax.experimental import pallas as pl
from jax.experimental.pallas import tpu as pltpu

D_MODEL = 1024
N_META = 16
D_A = 512
S5_GROUP = 16
G_A = D_A // S5_GROUP
S5_STATE = 64
D_B = 512
H_B = 8
BW_B = D_B // H_B
LRU_C = 8.0
CONV_B = 4
CONV_B_LEFT = 2
HEAD_DIM = 64
N_Q_HEADS = 16
N_KV_HEADS = 4
GQ = N_Q_HEADS // N_KV_HEADS
WINDOW = 128
BLOCK = 128
D_FF = 2816
CONV_F = 3
EPS = 1e-6
NEG = -1e30

PAD = BLOCK - N_META
FRONT = PAD + N_META
S5_T = 16
S5_W = S5_T * S5_GROUP
LANES = 128
SUBLANES = 8
BF16_ROWS = 16
FFN_HALO = BF16_ROWS
FFN_TF = 256
LRU_CB = LANES
VMEM_LIMIT = 56 * 1024 * 1024

F32 = jnp.float32
BF16 = jnp.bfloat16


def _cparams(n_axes):
    return pltpu.CompilerParams(
        dimension_semantics=("arbitrary",) * n_axes, vmem_limit_bytes=VMEM_LIMIT)


def _row_tile(rows, lp, cap=512, mult=BF16_ROWS):
    best = None
    t = mult
    while t <= min(cap, lp):
        if rows % t == 0:
            best = t
        t += mult
    assert best is not None, (rows, lp, cap, mult)
    return best


def _lru_tile(lp):
    q = lp // 32
    assert lp % 32 == 0
    odd = q
    while odd % 2 == 0:
        odd //= 2
    best = 1
    for d in range(1, odd + 1, 2):
        if odd % d == 0 and 32 * d <= 640:
            best = d
    return 32 * best


def _rmsnorm(x, g):
    ms = jnp.mean(x * x, axis=-1, keepdims=True)
    return x * lax.rsqrt(ms + EPS) * g


def _gelu(x):
    return x * (0.5 * (1.0 + jnp.tanh(math.sqrt(2.0 / math.pi) * (x + 0.044715 * (x * x * x)))))


def _sigmoid(x):
    return 1.0 / (1.0 + jnp.exp(-x))


def _valid_rows(tile_idx, tm, lp, shape):
    pos = lax.rem(tile_idx * tm, lp) + lax.broadcasted_iota(jnp.int32, shape, 0)
    pos = jnp.where(pos >= lp, pos - lp, pos)
    return pos >= PAD


def _norm_mm_kernel(h_ref, g_ref, w_ref, *o_refs, scales):
    hn = _rmsnorm(h_ref[...], g_ref[...]).astype(BF16)
    off = 0
    for o_ref, sc in zip(o_refs, scales):
        n = o_ref.shape[-1]
        z = jnp.dot(hn, w_ref[:, off:off + n], preferred_element_type=F32)
        if sc != 1.0:
            z = z * sc
        o_ref[...] = z.astype(o_ref.dtype)
        off += n


def _norm_mm(h2, g, w, splits, scales, lp):
    rows, d = h2.shape
    tm = _row_tile(rows, lp)
    n = w.shape[1]
    return pl.pallas_call(
        functools.partial(_norm_mm_kernel, scales=scales),
        grid=(rows // tm,),
        in_specs=[pl.BlockSpec((tm, d), lambda i: (i, 0)),
                  pl.BlockSpec((1, d), lambda i: (0, 0)),
                  pl.BlockSpec((d, n), lambda i: (0, 0))],
        out_specs=[pl.BlockSpec((tm, s), lambda i: (i, 0)) for s in splits],
        out_shape=[jax.ShapeDtypeStruct((rows, s), BF16) for s in splits],
        compiler_params=_cparams(1),
        name="norm_mm",
    )(h2, g.reshape(1, d), w)


UNIT = S5_T * BF16_ROWS
GRP_PER_VREG = LANES // S5_GROUP


def _unit_perm():
    r = jnp.arange(UNIT)
    src = (r % BF16_ROWS) * S5_T + r // BF16_ROWS
    return (src[:, None] == jnp.arange(UNIT)[None, :]).astype(BF16)


def _tokens_to_chunks(zp, store):
    grp = lax.broadcasted_iota(jnp.int32, (BF16_ROWS, LANES), 1) // S5_GROUP
    for j in range(D_A // LANES):
        v = [zp[s * BF16_ROWS:(s + 1) * BF16_ROWS, j * LANES:(j + 1) * LANES] for s in range(S5_T)]
        for gl in range(GRP_PER_VREG):
            for half in range(S5_T // GRP_PER_VREG):
                acc = None
                for p in range(GRP_PER_VREG):
                    src = v[half * GRP_PER_VREG + p]
                    shift = S5_GROUP * ((p - gl) % GRP_PER_VREG)
                    val = src if shift == 0 else pltpu.roll(src, shift, 1)
                    acc = val if acc is None else jnp.where(grp == p, val, acc)
                store(j * GRP_PER_VREG + gl, half, acc)


def _chunks_to_tokens(load):
    grp = lax.broadcasted_iota(jnp.int32, (BF16_ROWS, LANES), 1) // S5_GROUP
    cols = []
    for j in range(D_A // LANES):
        rows = []
        for half in range(S5_T // GRP_PER_VREG):
            y = [load(j * GRP_PER_VREG + gl, half) for gl in range(GRP_PER_VREG)]
            for p in range(GRP_PER_VREG):
                acc = None
                for gl in range(GRP_PER_VREG):
                    shift = S5_GROUP * ((gl - p) % GRP_PER_VREG)
                    val = y[gl] if shift == 0 else pltpu.roll(y[gl], shift, 1)
                    acc = val if acc is None else jnp.where(grp == gl, val, acc)
                rows.append(acc)
        cols.append(jnp.concatenate(rows, axis=0))
    return jnp.concatenate(cols, axis=1)


def _in_proj_kernel(h_ref, g_ref, w_ref, perm_ref, u_ref, xb_ref, gb_ref, *, tm):
    hn = _rmsnorm(h_ref[...], g_ref[...]).astype(BF16)
    xb_ref[...] = jnp.dot(hn, w_ref[:, D_A:D_A + D_B], preferred_element_type=F32).astype(BF16)
    gb_ref[...] = jnp.dot(hn, w_ref[:, D_A + D_B:], preferred_element_type=F32).astype(BF16)
    za = jnp.dot(hn, w_ref[:, 0:D_A], preferred_element_type=F32).astype(BF16)
    for q in range(tm // UNIT):
        zp = jnp.dot(perm_ref[...], za[q * UNIT:(q + 1) * UNIT], preferred_element_type=F32)

        def store(g, half, x, q=q):
            u_ref[g, q * BF16_ROWS:(q + 1) * BF16_ROWS, half * LANES:(half + 1) * LANES] = x.astype(BF16)
        _tokens_to_chunks(zp, store)


def _in_proj(h2, g, w, lp):
    rows, d = h2.shape
    tm = _row_tile(rows, lp, mult=UNIT)
    n = w.shape[1]
    return pl.pallas_call(
        functools.partial(_in_proj_kernel, tm=tm),
        grid=(rows // tm,),
        in_specs=[pl.BlockSpec((tm, d), lambda i: (i, 0)),
                  pl.BlockSpec((1, d), lambda i: (0, 0)),
                  pl.BlockSpec((d, n), lambda i: (0, 0)),
                  pl.BlockSpec((UNIT, UNIT), lambda i: (0, 0))],
        out_specs=[pl.BlockSpec((G_A, tm // S5_T, S5_W), lambda i: (0, i, 0)),
                   pl.BlockSpec((tm, D_B), lambda i: (i, 0)),
                   pl.BlockSpec((tm, D_B), lambda i: (i, 0))],
        out_shape=[jax.ShapeDtypeStruct((G_A, rows // S5_T, S5_W), BF16),
                   jax.ShapeDtypeStruct((rows, D_B), BF16),
                   jax.ShapeDtypeStruct((rows, D_B), BF16)],
        compiler_params=_cparams(1),
        name="in_proj",
    )(h2, g.reshape(1, d), w, _unit_perm())


def _cmul(ar, ai, br, bi):
    return ar * br - ai * bi, ar * bi + ai * br


def _s5_weights(lam_re, lam_im, log_dt, b_re, b_im, c_re, c_im, d_skip):
    hp = lax.Precision.HIGHEST
    t = S5_T
    lr = jnp.minimum(lam_re.astype(F32), -1e-4)
    li = lam_im.astype(F32)
    dt = jnp.exp(log_dt.astype(F32))[..., None]
    mag = jnp.exp(lr * dt)
    lbr, lbi = mag * jnp.cos(li * dt), mag * jnp.sin(li * dt)
    den = lr * lr + li * li
    xr, xi = lbr - 1.0, lbi
    fr, fi = (xr * lr + xi * li) / den, (xi * lr - xr * li) / den
    bbr, bbi = _cmul(fr[..., None], fi[..., None], b_re.astype(F32), b_im.astype(F32))
    pr, pi = [jnp.ones_like(lbr)], [jnp.zeros_like(lbi)]
    for _ in range(t):
        nr, ni = _cmul(pr[-1], pi[-1], lbr, lbi)
        pr.append(nr)
        pi.append(ni)
    pwr, pwi = jnp.stack(pr, -1), jnp.stack(pi, -1)
    cr, ci = c_re.astype(F32), c_im.astype(F32)
    wr, wi = _cmul(cr[..., None], ci[..., None], pwr[:, :, None], pwi[:, :, None])
    kk = (jnp.einsum('dgonk,dgni->dgkio', wr, bbr, precision=hp)
          - jnp.einsum('dgonk,dgni->dgkio', wi, bbi, precision=hp))[:, :, :t]
    eye = jnp.eye(S5_GROUP, dtype=F32) * d_skip.astype(F32).reshape(G_A, 1, S5_GROUP)
    k_all = jnp.concatenate([kk[1][:, 1:][:, ::-1],
                             (kk[0][:, 0] + kk[1][:, 0] + eye)[:, None],
                             kk[0][:, 1:]], axis=1)
    sidx = jnp.arange(t)
    lag = sidx[None, :] - sidx[:, None] + (t - 1)
    m = k_all[:, lag]
    m = m.transpose(0, 1, 3, 2, 4).reshape(G_A, S5_W, S5_W)
    pfr, pfi = _cmul(pwr[0][..., :t][..., ::-1][..., None], pwi[0][..., :t][..., ::-1][..., None],
                     bbr[0][:, :, None, :], bbi[0][:, :, None, :])
    pbr, pbi = _cmul(pwr[1][..., :t][..., None], pwi[1][..., :t][..., None],
                     bbr[1][:, :, None, :], bbi[1][:, :, None, :])

    def p_cols(x):
        x = x.transpose(0, 2, 3, 1).reshape(G_A, S5_W, S5_STATE)
        return jnp.pad(x, ((0, 0), (0, 0), (0, LANES - S5_STATE)))
    p = jnp.concatenate([p_cols(pfr), p_cols(pfi), p_cols(pbr), p_cols(pbi)], axis=-1)

    def q_rows(x):
        x = x.transpose(0, 2, 3, 1).reshape(G_A, S5_STATE, S5_W)
        return jnp.pad(x, ((0, 0), (0, LANES - S5_STATE), (0, 0)))
    q = jnp.concatenate([q_rows(wr[0][..., 1:]), q_rows(-wi[0][..., 1:]),
                         q_rows(wr[1][..., 1:][..., ::-1]), q_rows(-wi[1][..., 1:][..., ::-1])], axis=1)

    def c_row(x):
        return jnp.pad(x, ((0, 0), (0, LANES - S5_STATE)))[:, None]
    coef = jnp.concatenate([c_row(pwr[0][..., t]), c_row(pwi[0][..., t]),
                            c_row(pwr[1][..., t]), c_row(pwi[1][..., t]),
                            jnp.zeros((G_A, 4, LANES), F32)], axis=1)
    return m.astype(BF16), p.astype(BF16), q.astype(BF16), coef


def _s5_kernel(u_ref, m_ref, p_ref, q_ref, coef_ref, y_ref, s_scr, x0_scr, *, n_chunks, bsz, rb):
    rows = n_chunks * bsz
    ncomp = 4
    for k in range(rows // rb):
        sl = slice(k * rb, (k + 1) * rb)
        s = jnp.dot(u_ref[sl, :], p_ref[...], preferred_element_type=F32)
        for c in range(ncomp):
            s_scr[c, sl, :] = s[:, c * LANES:(c + 1) * LANES]

    shp = (bsz, LANES)
    lfr = jnp.broadcast_to(coef_ref[0:1, :], shp)
    lfi = jnp.broadcast_to(coef_ref[1:2, :], shp)
    lbr = jnp.broadcast_to(coef_ref[2:3, :], shp)
    lbi = jnp.broadcast_to(coef_ref[3:4, :], shp)

    def step(j, carry):
        xfr, xfi, xbr, xbi = carry
        rf = pl.ds(j, bsz, stride=n_chunks)
        rbk = pl.ds(n_chunks - 1 - j, bsz, stride=n_chunks)
        x0_scr[0, rf, :] = xfr
        x0_scr[1, rf, :] = xfi
        x0_scr[2, rbk, :] = xbr
        x0_scr[3, rbk, :] = xbi
        nfr = lfr * xfr - lfi * xfi + s_scr[0, rf, :]
        nfi = lfr * xfi + lfi * xfr + s_scr[1, rf, :]
        nbr = lbr * xbr - lbi * xbi + s_scr[2, rbk, :]
        nbi = lbr * xbi + lbi * xbr + s_scr[3, rbk, :]
        return nfr, nfi, nbr, nbi

    z = jnp.zeros(shp, F32)
    lax.fori_loop(0, n_chunks, step, (z, z, z, z))

    for k in range(rows // rb):
        sl = slice(k * rb, (k + 1) * rb)
        x0 = jnp.concatenate([x0_scr[c, sl, :] for c in range(ncomp)], axis=1).astype(BF16)
        y = jnp.dot(u_ref[sl, :], m_ref[...], preferred_element_type=F32)
        y = y + jnp.dot(x0, q_ref[...], preferred_element_type=F32)
        y_ref[sl, :] = y.astype(y_ref.dtype)


def _s5_mixer(ut, weights, bsz, lp):
    m, p, q, coef = weights
    n_chunks = lp // S5_T
    rows = n_chunks * bsz
    rb = _row_tile(rows, rows, cap=640)
    return pl.pallas_call(
        functools.partial(_s5_kernel, n_chunks=n_chunks, bsz=bsz, rb=rb),
        grid=(G_A,),
        in_specs=[pl.BlockSpec((None, rows, S5_W), lambda g: (g, 0, 0)),
                  pl.BlockSpec((None, S5_W, S5_W), lambda g: (g, 0, 0)),
                  pl.BlockSpec((None, S5_W, 4 * LANES), lambda g: (g, 0, 0)),
                  pl.BlockSpec((None, 4 * LANES, S5_W), lambda g: (g, 0, 0)),
                  pl.BlockSpec((None, SUBLANES, LANES), lambda g: (g, 0, 0))],
        out_specs=pl.BlockSpec((None, rows, S5_W), lambda g: (g, 0, 0)),
        out_shape=jax.ShapeDtypeStruct((G_A, rows, S5_W), BF16),
        scratch_shapes=[pltpu.VMEM((4, rows, LANES), F32), pltpu.VMEM((4, rows, LANES), F32)],
        compiler_params=_cparams(1),
        name="s5_chunks",
    )(ut, m, p, q, coef)


def _lru_weights(conv_w, conv_b, w_r, b_r, w_i, b_i, lam):
    ncb = D_B // LRU_CB
    hpb = LRU_CB // BW_B

    def blockdiag(w):
        w = w.astype(F32).reshape(2, ncb, hpb, BW_B, BW_B)
        out = jnp.zeros((2, ncb, hpb, BW_B, hpb, BW_B), F32)
        for k in range(hpb):
            out = out.at[:, :, k, :, k, :].set(w[:, :, k])
        return out.reshape(2, ncb, LRU_CB, LRU_CB)
    wg = jnp.concatenate([blockdiag(w_r), blockdiag(w_i)], axis=-1).astype(BF16)
    bg = jnp.concatenate([b_r.astype(F32).reshape(2, ncb, 1, LRU_CB),
                          b_i.astype(F32).reshape(2, ncb, 1, LRU_CB)], axis=-1)
    cl = (-LRU_C * jax.nn.softplus(-lam.astype(F32))).reshape(2, ncb, 1, LRU_CB)
    cw = jnp.pad(conv_w.astype(F32), ((0, SUBLANES - CONV_B), (0, 0)))
    cw = cw.reshape(SUBLANES, ncb, LRU_CB).transpose(1, 0, 2)
    cb = conv_b.astype(F32).reshape(ncb, 1, LRU_CB)
    return cw, cb, wg, bg, cl


def _lru_kernel(x_ref, g_ref, cw_ref, cb_ref, wg_ref, bg_ref, cl_ref, o_ref,
                xext, a_scr, b_scr, hf_scr, hb_scr, init_scr, *, lp, tt):
    seg = tt // SUBLANES
    n_tiles = lp // tt
    sb = tt // 4
    halo = SUBLANES
    zrow = jnp.zeros((halo, LANES), F32)
    xext[0:halo, :] = zrow
    xext[halo:halo + lp, :] = x_ref[...].astype(F32)
    xext[halo + lp:2 * halo + lp, :] = zrow

    def gates(d, t0):
        n = sb + 2 * halo
        for k in range(tt // sb):
            base = pl.multiple_of(t0 + k * sb, SUBLANES)
            win = xext[pl.ds(base, n), :]
            xc = (cw_ref[0:1, :] * pltpu.roll(win, 2, 0)[halo:halo + sb]
                  + cw_ref[1:2, :] * pltpu.roll(win, 1, 0)[halo:halo + sb]
                  + cw_ref[2:3, :] * win[halo:halo + sb]
                  + cw_ref[3:4, :] * pltpu.roll(win, n - 1, 0)[halo:halo + sb]) + cb_ref[...]
            z = jnp.dot(xc.astype(BF16), wg_ref[d], preferred_element_type=F32) + bg_ref[d]
            r = _sigmoid(z[:, :LRU_CB])
            ig = _sigmoid(z[:, LRU_CB:])
            a = jnp.exp(cl_ref[d] * r)
            inp = jnp.sqrt(1.0 - a * a) * (ig * xc)
            row = base + lax.broadcasted_iota(jnp.int32, (sb, LANES), 0)
            inp = jnp.where(row >= PAD, inp, 0.0)
            a_scr[d, k * sb:(k + 1) * sb, :] = a
            b_scr[d, k * sb:(k + 1) * sb, :] = inp

    a_f, b_f, a_b, b_b = a_scr.at[0], b_scr.at[0], a_scr.at[1], b_scr.at[1]

    def tile(jt, carry):
        hf_in, hb_in = carry
        t0f = jt * tt
        t0b = (n_tiles - 1 - jt) * tt
        gates(0, t0f)
        gates(1, t0b)

        def p1(i, c):
            hf, af, hb, ab = c
            rf = pl.ds(i, SUBLANES, stride=seg)
            rb = pl.ds(seg - 1 - i, SUBLANES, stride=seg)
            a1, a2 = a_f[rf, :], a_b[rb, :]
            return a1 * hf + b_f[rf, :], a1 * af, a2 * hb + b_b[rb, :], a2 * ab

        z8 = jnp.zeros((SUBLANES, LANES), F32)
        o8 = jnp.ones((SUBLANES, LANES), F32)
        hfe, afe, hbe, abe = lax.fori_loop(0, seg, p1, (z8, o8, z8, o8))
        c = hf_in
        for s in range(SUBLANES):
            init_scr[0, s:s + 1, :] = c
            c = hfe[s:s + 1] + afe[s:s + 1] * c
        hf_out = c
        c = hb_in
        for s in reversed(range(SUBLANES)):
            init_scr[1, s:s + 1, :] = c
            c = hbe[s:s + 1] + abe[s:s + 1] * c
        hb_out = c

        def p2(i, c):
            hf, hb = c
            rf = pl.ds(i, SUBLANES, stride=seg)
            rb = pl.ds(seg - 1 - i, SUBLANES, stride=seg)
            hf = a_f[rf, :] * hf + b_f[rf, :]
            hb = a_b[rb, :] * hb + b_b[rb, :]
            hf_scr[pl.ds(t0f + i, SUBLANES, stride=seg), :] = hf
            hb_scr[pl.ds(t0b + seg - 1 - i, SUBLANES, stride=seg), :] = hb
            return hf, hb

        lax.fori_loop(0, seg, p2, (init_scr[0], init_scr[1]))
        return hf_out, hb_out

    z1 = jnp.zeros((1, LANES), F32)
    lax.fori_loop(0, n_tiles, tile, (z1, z1))

    cbk = tt // 2

    def combine(k, _):
        rs = pl.ds(pl.multiple_of(k * cbk, BF16_ROWS), cbk)
        h = hf_scr[rs, :] + hb_scr[rs, :]
        o_ref[rs, :] = (h * _gelu(g_ref[rs, :].astype(F32))).astype(o_ref.dtype)
        return 0

    lax.fori_loop(0, lp // cbk, combine, 0)


def _lru_mixer(x_b, g_b, weights, bsz, lp):
    cw, cb, wg, bg, cl = weights
    ncb = D_B // LRU_CB
    tt = _lru_tile(lp)
    x3 = x_b.reshape(bsz, lp, D_B)
    g3 = g_b.reshape(bsz, lp, D_B)
    seq_spec = pl.BlockSpec((None, lp, LRU_CB), lambda b, c: (b, 0, c))
    out = pl.pallas_call(
        functools.partial(_lru_kernel, lp=lp, tt=tt),
        grid=(bsz, ncb),
        in_specs=[seq_spec, seq_spec,
                  pl.BlockSpec((None, SUBLANES, LRU_CB), lambda b, c: (c, 0, 0)),
                  pl.BlockSpec((None, 1, LRU_CB), lambda b, c: (c, 0, 0)),
                  pl.BlockSpec((2, None, LRU_CB, 2 * LRU_CB), lambda b, c: (0, c, 0, 0)),
                  pl.BlockSpec((2, None, 1, 2 * LRU_CB), lambda b, c: (0, c, 0, 0)),
                  pl.BlockSpec((2, None, 1, LRU_CB), lambda b, c: (0, c, 0, 0))],
        out_specs=seq_spec,
        out_shape=jax.ShapeDtypeStruct((bsz, lp, D_B), BF16),
        scratch_shapes=[pltpu.VMEM((lp + 2 * SUBLANES, LANES), F32),
                        pltpu.VMEM((2, tt, LANES), F32), pltpu.VMEM((2, tt, LANES), F32),
                        pltpu.VMEM((lp, LANES), F32), pltpu.VMEM((lp, LANES), F32),
                        pltpu.VMEM((2, SUBLANES, LANES), F32)],
        compiler_params=_cparams(2),
        name="rglru",
    )(x3, g3, cw, cb, wg, bg, cl)
    return out.reshape(bsz * lp, D_B)


def _ab_out_kernel(yt_ref, yb_ref, h_ref, permt_ref, wglu_ref, bglu_ref, wo_ref, o_ref, *, lp, tm):
    units = []
    for q in range(tm // UNIT):
        def load(g, half, q=q):
            return yt_ref[g, q * BF16_ROWS:(q + 1) * BF16_ROWS, half * LANES:(half + 1) * LANES].astype(F32)
        ysc = _chunks_to_tokens(load).astype(BF16)
        units.append(jnp.dot(permt_ref[...], ysc, preferred_element_type=F32))
    ya = _gelu(jnp.concatenate(units, axis=0))
    gate = _sigmoid(jnp.dot(ya.astype(BF16), wglu_ref[...], preferred_element_type=F32) + bglu_ref[...])
    ya = (ya * gate).astype(BF16)
    acc = jnp.dot(ya, wo_ref[0:D_A, :], preferred_element_type=F32)
    acc = acc + jnp.dot(yb_ref[...], wo_ref[D_A:, :], preferred_element_type=F32)
    valid = _valid_rows(pl.program_id(0), tm, lp, acc.shape)
    o_ref[...] = jnp.where(valid, h_ref[...] + acc, 0.0)


def _ab_out(yt, yb, h2, w_glu, b_glu, w_out, lp):
    rows, d = h2.shape
    tm = _row_tile(rows, lp, mult=UNIT)
    return pl.pallas_call(
        functools.partial(_ab_out_kernel, lp=lp, tm=tm),
        grid=(rows // tm,),
        in_specs=[pl.BlockSpec((G_A, tm // S5_T, S5_W), lambda i: (0, i, 0)),
                  pl.BlockSpec((tm, D_B), lambda i: (i, 0)),
                  pl.BlockSpec((tm, d), lambda i: (i, 0)),
                  pl.BlockSpec((UNIT, UNIT), lambda i: (0, 0)),
                  pl.BlockSpec((D_A, D_A), lambda i: (0, 0)),
                  pl.BlockSpec((1, D_A), lambda i: (0, 0)),
                  pl.BlockSpec((D_A + D_B, d), lambda i: (0, 0))],
        out_specs=pl.BlockSpec((tm, d), lambda i: (i, 0)),
        out_shape=jax.ShapeDtypeStruct((rows, d), F32),
        compiler_params=_cparams(1),
        name="ab_out",
    )(yt, yb, h2, _unit_perm().T, w_glu.astype(BF16), b_glu.astype(F32).reshape(1, D_A), w_out.astype(BF16))


def _proj_res_kernel(x_ref, h_ref, w_ref, o_ref, *, lp, tm):
    acc = jnp.dot(x_ref[...], w_ref[...], preferred_element_type=F32)
    valid = _valid_rows(pl.program_id(0), tm, lp, acc.shape)
    o_ref[...] = jnp.where(valid, h_ref[...] + acc, 0.0)


def _proj_res(x, h2, w, lp):
    rows, d = h2.shape
    tm = _row_tile(rows, lp)
    k = x.shape[1]
    return pl.pallas_call(
        functools.partial(_proj_res_kernel, lp=lp, tm=tm),
        grid=(rows // tm,),
        in_specs=[pl.BlockSpec((tm, k), lambda i: (i, 0)),
                  pl.BlockSpec((tm, d), lambda i: (i, 0)),
                  pl.BlockSpec((k, d), lambda i: (0, 0))],
        out_specs=pl.BlockSpec((tm, d), lambda i: (i, 0)),
        out_shape=jax.ShapeDtypeStruct((rows, d), F32),
        compiler_params=_cparams(1),
        name="proj_res",
    )(x, h2, w.astype(BF16))


LOG2E = math.log2(math.e)
N_PAIRS = N_Q_HEADS // 2


def _pair_heads():
    return [((2 * hp) * GQ + g, (2 * hp + 1) * GQ + g) for hp in range(N_KV_HEADS // 2) for g in range(GQ)]


def _attn_bias(lp):
    nb = lp // BLOCK
    assert nb >= 3
    qi = jnp.arange(BLOCK)[:, None]
    ki = jnp.arange(3 * BLOCK)[None, :]
    dist = jnp.abs(qi + BLOCK - ki)
    slopes = 2.0 ** (-8.0 * jnp.arange(1, N_Q_HEADS + 1, dtype=F32) / N_Q_HEADS)
    tables = []
    for n in (0, 1, 2 if nb > 3 else None, nb - 1):
        if n is None:
            tables.append(tables[-1])
            continue
        key_pos = (n - 1) * BLOCK + ki
        ok = (dist <= WINDOW) & (key_pos >= PAD) & (key_pos < lp)
        alibi = -slopes[:, None, None] * dist.astype(F32)[None] * LOG2E
        tables.append(jnp.where(ok[None], alibi, NEG))
    return jnp.stack(tables)


def _attn_kernel(q_ref, kp_ref, kc_ref, kn_ref, vp_ref, vc_ref, vn_ref, bias_ref, sink_ref, o_ref,
                 k_scr, v_scr, s_scr, m_scr, p_scr):
    low_kv = lax.broadcasted_iota(jnp.int32, (3 * BLOCK, LANES), 1) < HEAD_DIM
    low_q = lax.broadcasted_iota(jnp.int32, (BLOCK, LANES), 1) < HEAD_DIM
    heads = _pair_heads()
    n_hp = N_KV_HEADS // 2
    for hp in range(n_hp):
        cols = slice(hp * LANES, (hp + 1) * LANES)
        kpair = jnp.concatenate([kp_ref[:, cols], kc_ref[:, cols], kn_ref[:, cols]], axis=0)
        vpair = jnp.concatenate([vp_ref[:, cols], vc_ref[:, cols], vn_ref[:, cols]], axis=0)
        zero = jnp.zeros_like(kpair)
        one = jnp.ones_like(vpair)
        k_scr[hp, 0] = jnp.where(low_kv, kpair, zero)
        k_scr[hp, 1] = jnp.where(low_kv, zero, kpair)
        v_scr[hp, 0] = jnp.where(low_kv, vpair, one)
        v_scr[hp, 1] = jnp.where(low_kv, one, vpair)
    tiles = [(hp, g, part) for hp in range(n_hp) for g in range(GQ) for part in range(2)]
    for t, (hp, g, part) in enumerate(tiles):
        pair = hp * GQ + g
        qp = q_ref[:, pair * LANES:(pair + 1) * LANES]
        s = lax.dot_general(qp, k_scr[hp, part], (((1,), (1,)), ((), ())), preferred_element_type=F32)
        s_scr[t] = s + bias_ref[heads[pair][part]]
    for t, (hp, g, part) in enumerate(tiles):
        sink = sink_ref[0, heads[hp * GQ + g][part]]
        m = jnp.maximum(jnp.max(s_scr[t], axis=-1, keepdims=True), sink)
        m_scr[t] = jnp.broadcast_to(m, (BLOCK, LANES))
    for t in range(len(tiles)):
        m = m_scr[t]
        for c in range(3):
            cs = slice(c * BLOCK, (c + 1) * BLOCK)
            p_scr[t, :, cs] = jnp.exp2(s_scr[t, :, cs] - m).astype(BF16)
    for t, (hp, g, part) in enumerate(tiles):
        pair = hp * GQ + g
        sink = sink_ref[0, heads[pair][part]]
        oe = jnp.dot(p_scr[t], v_scr[hp, part], preferred_element_type=F32)
        o = oe / (pltpu.roll(oe, HEAD_DIM, 1) + jnp.exp2(sink - m_scr[t]))
        if part == 0:
            first = o
        else:
            o_ref[:, pair * LANES:(pair + 1) * LANES] = jnp.where(low_q, first, o).astype(o_ref.dtype)


def _attention(q, k, v, sink, bsz, lp):
    nb = lp // BLOCK
    dq = N_Q_HEADS * HEAD_DIM
    dkv = N_KV_HEADS * HEAD_DIM
    q3, k3, v3 = q.reshape(bsz, lp, dq), k.reshape(bsz, lp, dkv), v.reshape(bsz, lp, dkv)
    sink2 = (sink.astype(F32) * LOG2E).reshape(1, N_Q_HEADS)
    kv_prev = pl.BlockSpec((None, BLOCK, dkv), lambda b, n: (b, jnp.maximum(n - 1, 0), 0))
    kv_cur = pl.BlockSpec((None, BLOCK, dkv), lambda b, n: (b, n, 0))
    kv_next = pl.BlockSpec((None, BLOCK, dkv), lambda b, n: (b, jnp.minimum(n + 1, nb - 1), 0))

    def bias_case(b, n):
        return (jnp.where(n == 0, 0, jnp.where(n == 1, 1, jnp.where(n == nb - 1, 3, 2))), 0, 0, 0)
    out = pl.pallas_call(
        _attn_kernel,
        grid=(bsz, nb),
        in_specs=[pl.BlockSpec((None, BLOCK, dq), lambda b, n: (b, n, 0)),
                  kv_prev, kv_cur, kv_next, kv_prev, kv_cur, kv_next,
                  pl.BlockSpec((None, N_Q_HEADS, BLOCK, 3 * BLOCK), bias_case),
                  pl.BlockSpec(memory_space=pltpu.SMEM)],
        out_specs=pl.BlockSpec((None, BLOCK, dq), lambda b, n: (b, n, 0)),
        out_shape=jax.ShapeDtypeStruct((bsz, lp, dq), BF16),
        scratch_shapes=[pltpu.VMEM((N_KV_HEADS // 2, 2, 3 * BLOCK, LANES), BF16),
                        pltpu.VMEM((N_KV_HEADS // 2, 2, 3 * BLOCK, LANES), BF16),
                        pltpu.VMEM((N_Q_HEADS, BLOCK, 3 * BLOCK), F32),
                        pltpu.VMEM((N_Q_HEADS, BLOCK, LANES), F32),
                        pltpu.VMEM((N_Q_HEADS, BLOCK, 3 * BLOCK), BF16)],
        compiler_params=_cparams(2),
        name="swa",
    )(q3, k3, k3, k3, v3, v3, v3, _attn_bias(lp), sink2)
    return out.reshape(bsz * lp, dq)


def _ffn_kernel(hp_ref, hc_ref, hn_ref, g_ref, wup_ref, cw_ref, cb_ref, wdn_ref, gf_ref, o_ref,
                hx_scr, act_scr, *, lp, tm, n_tiles, final):
    i = pl.program_id(0)
    g = g_ref[...]
    hc = hc_ref[...]
    n = tm + 2 * FFN_HALO
    hx_scr[0:FFN_HALO, :] = _rmsnorm(hp_ref[...], g).astype(BF16)
    hx_scr[FFN_HALO:FFN_HALO + tm, :] = _rmsnorm(hc, g).astype(BF16)
    keep = jnp.where(i < n_tiles - 1, 1.0, 0.0)
    hx_scr[FFN_HALO + tm:n, :] = (_rmsnorm(hn_ref[...], g) * keep).astype(BF16)

    def conv(u, cols):
        lo, hi = FFN_HALO, FFN_HALO + tm
        return (cw_ref[0:1, cols] * pltpu.roll(u, 1, 0)[lo:hi]
                + cw_ref[1:2, cols] * u[lo:hi]
                + cw_ref[2:3, cols] * pltpu.roll(u, n - 1, 0)[lo:hi]) + cb_ref[0:1, cols]

    for c in range(D_FF // FFN_TF):
        ca = slice(c * FFN_TF, (c + 1) * FFN_TF)
        cg = slice(D_FF + c * FFN_TF, D_FF + (c + 1) * FFN_TF)
        ua = jnp.dot(hx_scr[...], wup_ref[:, ca], preferred_element_type=F32)
        ug = jnp.dot(hx_scr[...], wup_ref[:, cg], preferred_element_type=F32)
        act_scr[:, ca] = (_gelu(conv(ug, cg)) * conv(ua, ca)).astype(BF16)

    y = hc + jnp.dot(act_scr[...], wdn_ref[...], preferred_element_type=F32)
    if final:
        o_ref[...] = _rmsnorm(y, gf_ref[...])
    else:
        o_ref[...] = jnp.where(_valid_rows(i, tm, lp, y.shape), y, 0.0)


def _conv_ffn(h2, g, w_up, conv_w, conv_b, w_down, g_final, lp, final):
    rows, d = h2.shape
    tm = _row_tile(rows, lp)
    n_tiles = rows // tm
    hb = tm // FFN_HALO
    last = rows // FFN_HALO - 1
    const = dict(pipeline_mode=pl.Buffered(1))
    cw = jnp.pad(conv_w.astype(F32), ((0, SUBLANES - CONV_F), (0, 0)))
    return pl.pallas_call(
        functools.partial(_ffn_kernel, lp=lp, tm=tm, n_tiles=n_tiles, final=final),
        grid=(n_tiles,),
        in_specs=[pl.BlockSpec((FFN_HALO, d), lambda i: (jnp.maximum(i * hb - 1, 0), 0)),
                  pl.BlockSpec((tm, d), lambda i: (i, 0)),
                  pl.BlockSpec((FFN_HALO, d), lambda i: (jnp.minimum((i + 1) * hb, last), 0)),
                  pl.BlockSpec((1, d), lambda i: (0, 0)),
                  pl.BlockSpec((d, 2 * D_FF), lambda i: (0, 0), **const),
                  pl.BlockSpec((SUBLANES, 2 * D_FF), lambda i: (0, 0)),
                  pl.BlockSpec((1, 2 * D_FF), lambda i: (0, 0)),
                  pl.BlockSpec((D_FF, d), lambda i: (0, 0), **const),
                  pl.BlockSpec((1, d), lambda i: (0, 0))],
        out_specs=pl.BlockSpec((tm, d), lambda i: (i, 0)),
        out_shape=jax.ShapeDtypeStruct((rows, d), F32),
        scratch_shapes=[pltpu.VMEM((tm + 2 * FFN_HALO, d), BF16), pltpu.VMEM((tm, D_FF), BF16)],
        compiler_params=_cparams(1),
        name="conv_ffn",
    )(h2, h2, h2, g.astype(F32).reshape(1, d), w_up.astype(BF16), cw,
      conv_b.astype(F32).reshape(1, 2 * D_FF), w_down.astype(BF16), g_final.astype(F32).reshape(1, d))


def _trunk(x, prm):
    bsz, seq, d = x.shape
    lp = seq + FRONT
    assert seq % BLOCK == 0 and d == D_MODEL
    meta = jnp.broadcast_to(prm['meta_tokens'][None].astype(F32), (bsz, N_META, d))
    h = jnp.concatenate([jnp.zeros((bsz, PAD, d), F32), meta, x.astype(F32)], axis=1).reshape(bsz * lp, d)

    u_a, x_b, g_b = _in_proj(h, prm['norm_mix_g'][0], prm['w_in_ab'][0].astype(BF16), lp)
    s5w = _s5_weights(prm['s5_lambda_re'][0], prm['s5_lambda_im'][0], prm['s5_log_dt'][0],
                      prm['s5_b_re'][0], prm['s5_b_im'][0], prm['s5_c_re'][0], prm['s5_c_im'][0], prm['s5_d'][0])
    y_a = _s5_mixer(u_a, s5w, bsz, lp)
    lruw = _lru_weights(prm['lru_conv_w'][0], prm['lru_conv_b'][0], prm['lru_w_r'][0], prm['lru_b_r'][0],
                        prm['lru_w_i'][0], prm['lru_b_i'][0], prm['lru_lambda'][0])
    y_b = _lru_mixer(x_b, g_b, lruw, bsz, lp)
    h = _ab_out(y_a, y_b, h, prm['w_glu'][0], prm['b_glu'][0], prm['w_out_ab'][0], lp)
    h = _conv_ffn(h, prm['norm_ffn_g'][0], prm['w_up'][0], prm['ffn_conv_w'][0], prm['ffn_conv_b'][0],
                  prm['w_down'][0], prm['final_norm_g'], lp, final=False)

    dq, dkv = N_Q_HEADS * HEAD_DIM, N_KV_HEADS * HEAD_DIM
    head_order = jnp.asarray([h_ for pair in _pair_heads() for h_ in pair])
    col_order = (head_order[:, None] * HEAD_DIM + jnp.arange(HEAD_DIM)[None, :]).reshape(-1)
    w_qkv = prm['w_qkv'][0]
    w_qkv = jnp.concatenate([w_qkv[:, :dq][:, col_order], w_qkv[:, dq:]], axis=1).astype(BF16)
    q, k, v = _norm_mm(h, prm['norm_mix_g'][1], w_qkv, (dq, dkv, dkv),
                       (LOG2E / math.sqrt(HEAD_DIM), 1.0, 1.0), lp)
    o = _attention(q, k, v, prm['attn_sink'][0], bsz, lp)
    h = _proj_res(o, h, prm['w_o'][0][col_order], lp)
    h = _conv_ffn(h, prm['norm_ffn_g'][1], prm['w_up'][1], prm['ffn_conv_w'][1], prm['ffn_conv_b'][1],
                  prm['w_down'][1], prm['final_norm_g'], lp, final=True)
    return h.reshape(bsz, lp, d)[:, FRONT:]


def kernel(x_prompt, x_sample, meta_tokens, norm_mix_g, norm_ffn_g, final_norm_g, w_in_ab, s5_lambda_re, s5_lambda_im, s5_log_dt, s5_b_re, s5_b_im, s5_c_re, s5_c_im, s5_d, w_glu, b_glu, lru_conv_w, lru_conv_b, lru_w_r, lru_b_r, lru_w_i, lru_b_i, lru_lambda, w_out_ab, w_qkv, w_o, attn_sink, w_up, ffn_conv_w, ffn_conv_b, w_down):
    prm = dict(meta_tokens=meta_tokens, norm_mix_g=norm_mix_g, norm_ffn_g=norm_ffn_g, final_norm_g=final_norm_g,
               w_in_ab=w_in_ab, s5_lambda_re=s5_lambda_re, s5_lambda_im=s5_lambda_im, s5_log_dt=s5_log_dt,
               s5_b_re=s5_b_re, s5_b_im=s5_b_im, s5_c_re=s5_c_re, s5_c_im=s5_c_im, s5_d=s5_d, w_glu=w_glu,
               b_glu=b_glu, lru_conv_w=lru_conv_w, lru_conv_b=lru_conv_b, lru_w_r=lru_w_r, lru_b_r=lru_b_r,
               lru_w_i=lru_w_i, lru_b_i=lru_b_i, lru_lambda=lru_lambda, w_out_ab=w_out_ab, w_qkv=w_qkv,
               w_o=w_o, attn_sink=attn_sink, w_up=w_up, ffn_conv_w=ffn_conv_w, ffn_conv_b=ffn_conv_b,
               w_down=w_down)
    return (_trunk(x_prompt, prm), _trunk(x_sample, prm))
```

```python
import functools
import math

import jax
import jax.numpy as jnp
from jax import lax
from jax.experimental import pallas as pl
from jax.experimental.pallas import tpu as pltpu

D_MODEL = 1024
N_META = 16
D_A = 512
S5_GROUP = 16
G_A = D_A // S5_GROUP
S5_STATE = 64
D_B = 512
H_B = 8
BW_B = D_B // H_B
LRU_C = 8.0
CONV_B = 4
CONV_B_LEFT = 2
HEAD_DIM = 64
N_Q_HEADS = 16
N_KV_HEADS = 4
GQ = N_Q_HEADS // N_KV_HEADS
WINDOW = 128
BLOCK = 128
D_FF = 2816
CONV_F = 3
EPS = 1e-6
NEG = -1e30

PAD = BLOCK - N_META
FRONT = PAD + N_META
S5_T = 16
S5_W = S5_T * S5_GROUP
LANES = 128
SUBLANES = 8
BF16_ROWS = 16
FFN_HALO = BF16_ROWS
FFN_TF = 256
LRU_CB = LANES
VMEM_LIMIT = 56 * 1024 * 1024

F32 = jnp.float32
BF16 = jnp.bfloat16


def _cparams(n_axes):
    return pltpu.CompilerParams(
        dimension_semantics=("arbitrary",) * n_axes, vmem_limit_bytes=VMEM_LIMIT)


def _row_tile(rows, lp, cap=512, mult=BF16_ROWS):
    best = None
    t = mult
    while t <= min(cap, lp):
        if rows % t == 0:
            best = t
        t += mult
    assert best is not None, (rows, lp, cap, mult)
    return best


def _lru_tile(lp):
    q = lp // 32
    assert lp % 32 == 0
    odd = q
    while odd % 2 == 0:
        odd //= 2
    best = 1
    for d in range(1, odd + 1, 2):
        if odd % d == 0 and 32 * d <= 640:
            best = d
    return 32 * best


def _rmsnorm(x, g):
    ms = jnp.mean(x * x, axis=-1, keepdims=True)
    return x * lax.rsqrt(ms + EPS) * g


def _gelu(x):
    k1 = -2.0 * math.sqrt(2.0 / math.pi) * math.log2(math.e)
    return x / (1.0 + jnp.exp2(x * (k1 + (k1 * 0.044715) * (x * x))))


def _sigmoid(x):
    return 1.0 / (1.0 + jnp.exp(-x))


def _valid_rows(tile_idx, tm, lp, shape):
    pos = lax.rem(tile_idx * tm, lp) + lax.broadcasted_iota(jnp.int32, shape, 0)
    pos = jnp.where(pos >= lp, pos - lp, pos)
    return pos >= PAD


def _norm_mm_kernel(h_ref, g_ref, w_ref, *o_refs, scales):
    hn = _rmsnorm(h_ref[...], g_ref[...]).astype(BF16)
    off = 0
    for o_ref, sc in zip(o_refs, scales):
        n = o_ref.shape[-1]
        z = jnp.dot(hn, w_ref[:, off:off + n], preferred_element_type=F32)
        if sc != 1.0:
            z = z * sc
        o_ref[...] = z.astype(o_ref.dtype)
        off += n


def _norm_mm(h2, g, w, splits, scales, lp):
    rows, d = h2.shape
    tm = _row_tile(rows, lp)
    n = w.shape[1]
    return pl.pallas_call(
        functools.partial(_norm_mm_kernel, scales=scales),
        grid=(rows // tm,),
        in_specs=[pl.BlockSpec((tm, d), lambda i: (i, 0)),
                  pl.BlockSpec((1, d), lambda i: (0, 0)),
                  pl.BlockSpec((d, n), lambda i: (0, 0))],
        out_specs=[pl.BlockSpec((tm, s), lambda i: (i, 0)) for s in splits],
        out_shape=[jax.ShapeDtypeStruct((rows, s), BF16) for s in splits],
        compiler_params=_cparams(1),
        name="norm_mm",
    )(h2, g.reshape(1, d), w)


UNIT = S5_T * BF16_ROWS
GRP_PER_VREG = LANES // S5_GROUP


def _unit_perm():
    r = jnp.arange(UNIT)
    src = (r % BF16_ROWS) * S5_T + r // BF16_ROWS
    return (src[:, None] == jnp.arange(UNIT)[None, :]).astype(BF16)


def _tokens_to_chunks(zp, store):
    grp = lax.broadcasted_iota(jnp.int32, (BF16_ROWS, LANES), 1) // S5_GROUP
    for j in range(D_A // LANES):
        v = [zp[s * BF16_ROWS:(s + 1) * BF16_ROWS, j * LANES:(j + 1) * LANES] for s in range(S5_T)]
        for gl in range(GRP_PER_VREG):
            for half in range(S5_T // GRP_PER_VREG):
                acc = None
                for p in range(GRP_PER_VREG):
                    src = v[half * GRP_PER_VREG + p]
                    shift = S5_GROUP * ((p - gl) % GRP_PER_VREG)
                    val = src if shift == 0 else pltpu.roll(src, shift, 1)
                    acc = val if acc is None else jnp.where(grp == p, val, acc)
                store(j * GRP_PER_VREG + gl, half, acc)


def _chunks_to_tokens(load):
    grp = lax.broadcasted_iota(jnp.int32, (BF16_ROWS, LANES), 1) // S5_GROUP
    cols = []
    for j in range(D_A // LANES):
        rows = []
        for half in range(S5_T // GRP_PER_VREG):
            y = [load(j * GRP_PER_VREG + gl, half) for gl in range(GRP_PER_VREG)]
            for p in range(GRP_PER_VREG):
                acc = None
                for gl in range(GRP_PER_VREG):
                    shift = S5_GROUP * ((gl - p) % GRP_PER_VREG)
                    val = y[gl] if shift == 0 else pltpu.roll(y[gl], shift, 1)
                    acc = val if acc is None else jnp.where(grp == gl, val, acc)
                rows.append(acc)
        cols.append(jnp.concatenate(rows, axis=0))
    return jnp.concatenate(cols, axis=1)


def _in_proj_kernel(*refs, tm, nb):
    n_cur = tm // BLOCK
    x_refs = refs[:n_cur]
    meta_ref, g_ref, w_ref, perm_ref, h_ref, u_ref, xb_ref, gb_ref = refs[n_cur:]
    first = pl.program_id(0) * n_cur
    blocks = [jnp.where(lax.rem(first + k, nb) == 0, meta_ref[...], x_refs[k][...]) for k in range(n_cur)]
    h = jnp.concatenate(blocks, axis=0)
    h_ref[...] = h
    hn = _rmsnorm(h, g_ref[...]).astype(BF16)
    xb_ref[...] = jnp.dot(hn, w_ref[:, D_A:D_A + D_B], preferred_element_type=F32).astype(BF16)
    gb_ref[...] = jnp.dot(hn, w_ref[:, D_A + D_B:], preferred_element_type=F32).astype(BF16)
    za = jnp.dot(hn, w_ref[:, 0:D_A], preferred_element_type=F32).astype(BF16)
    for q in range(tm // UNIT):
        zp = jnp.dot(perm_ref[...], za[q * UNIT:(q + 1) * UNIT], preferred_element_type=F32)

        def store(g, half, x, q=q):
            u_ref[g, q * BF16_ROWS:(q + 1) * BF16_ROWS, half * LANES:(half + 1) * LANES] = x.astype(BF16)
        _tokens_to_chunks(zp, store)


def _in_proj(x, meta, g, w, lp):
    bsz, seq, d = x.shape
    rows = bsz * lp
    nb = lp // BLOCK
    tm = _row_tile(rows, lp, mult=UNIT)
    n_cur = tm // BLOCK
    n = w.shape[1]
    meta_blk = jnp.pad(meta.astype(F32), ((PAD, 0), (0, 0)))
    x_specs = [pl.BlockSpec((None, BLOCK, d),
                            lambda i, k=k: ((i * n_cur + k) // nb, jnp.maximum((i * n_cur + k) % nb - 1, 0), 0))
               for k in range(n_cur)]
    return pl.pallas_call(
        functools.partial(_in_proj_kernel, tm=tm, nb=nb),
        grid=(rows // tm,),
        in_specs=x_specs + [pl.BlockSpec((BLOCK, d), lambda i: (0, 0)),
                            pl.BlockSpec((1, d), lambda i: (0, 0)),
                            pl.BlockSpec((d, n), lambda i: (0, 0)),
                            pl.BlockSpec((UNIT, UNIT), lambda i: (0, 0))],
        out_specs=[pl.BlockSpec((tm, d), lambda i: (i, 0)),
                   pl.BlockSpec((G_A, tm // S5_T, S5_W), lambda i: (0, i, 0)),
                   pl.BlockSpec((tm, D_B), lambda i: (i, 0)),
                   pl.BlockSpec((tm, D_B), lambda i: (i, 0))],
        out_shape=[jax.ShapeDtypeStruct((rows, d), F32),
                   jax.ShapeDtypeStruct((G_A, rows // S5_T, S5_W), BF16),
                   jax.ShapeDtypeStruct((rows, D_B), BF16),
                   jax.ShapeDtypeStruct((rows, D_B), BF16)],
        compiler_params=_cparams(1),
        name="in_proj",
    )(*([x.astype(F32)] * n_cur), meta_blk, g.reshape(1, d), w, _unit_perm())


def _cmul(ar, ai, br, bi):
    return ar * br - ai * bi, ar * bi + ai * br


def _s5_weights(lam_re, lam_im, log_dt, b_re, b_im, c_re, c_im, d_skip):
    hp = lax.Precision.HIGHEST
    t = S5_T
    lr = jnp.minimum(lam_re.astype(F32), -1e-4)
    li = lam_im.astype(F32)
    dt = jnp.exp(log_dt.astype(F32))[..., None]
    mag = jnp.exp(lr * dt)
    lbr, lbi = mag * jnp.cos(li * dt), mag * jnp.sin(li * dt)
    den = lr * lr + li * li
    xr, xi = lbr - 1.0, lbi
    fr, fi = (xr * lr + xi * li) / den, (xi * lr - xr * li) / den
    bbr, bbi = _cmul(fr[..., None], fi[..., None], b_re.astype(F32), b_im.astype(F32))
    pr, pi = [jnp.ones_like(lbr)], [jnp.zeros_like(lbi)]
    for _ in range(t):
        nr, ni = _cmul(pr[-1], pi[-1], lbr, lbi)
        pr.append(nr)
        pi.append(ni)
    pwr, pwi = jnp.stack(pr, -1), jnp.stack(pi, -1)
    cr, ci = c_re.astype(F32), c_im.astype(F32)
    wr, wi = _cmul(cr[..., None], ci[..., None], pwr[:, :, None], pwi[:, :, None])
    kk = (jnp.einsum('dgonk,dgni->dgkio', wr, bbr, precision=hp)
          - jnp.einsum('dgonk,dgni->dgkio', wi, bbi, precision=hp))[:, :, :t]
    eye = jnp.eye(S5_GROUP, dtype=F32) * d_skip.astype(F32).reshape(G_A, 1, S5_GROUP)
    k_all = jnp.concatenate([kk[1][:, 1:][:, ::-1],
                             (kk[0][:, 0] + kk[1][:, 0] + eye)[:, None],
                             kk[0][:, 1:]], axis=1)
    sidx = jnp.arange(t)
    lag = sidx[None, :] - sidx[:, None] + (t - 1)
    m = k_all[:, lag]
    m = m.transpose(0, 1, 3, 2, 4).reshape(G_A, S5_W, S5_W)
    pfr, pfi = _cmul(pwr[0][..., :t][..., ::-1][..., None], pwi[0][..., :t][..., ::-1][..., None],
                     bbr[0][:, :, None, :], bbi[0][:, :, None, :])
    pbr, pbi = _cmul(pwr[1][..., :t][..., None], pwi[1][..., :t][..., None],
                     bbr[1][:, :, None, :], bbi[1][:, :, None, :])

    def p_cols(x):
        x = x.transpose(0, 2, 3, 1).reshape(G_A, S5_W, S5_STATE)
        return jnp.pad(x, ((0, 0), (0, 0), (0, LANES - S5_STATE)))
    p = jnp.concatenate([p_cols(pfr), p_cols(pfi), p_cols(pbr), p_cols(pbi)], axis=-1)

    def q_rows(x):
        x = x.transpose(0, 2, 3, 1).reshape(G_A, S5_STATE, S5_W)
        return jnp.pad(x, ((0, 0), (0, LANES - S5_STATE), (0, 0)))
    q = jnp.concatenate([q_rows(wr[0][..., 1:]), q_rows(-wi[0][..., 1:]),
                         q_rows(wr[1][..., 1:][..., ::-1]), q_rows(-wi[1][..., 1:][..., ::-1])], axis=1)

    def c_row(x):
        return jnp.pad(x, ((0, 0), (0, LANES - S5_STATE)))[:, None]
    coef = jnp.concatenate([c_row(pwr[0][..., t]), c_row(pwi[0][..., t]),
                            c_row(pwr[1][..., t]), c_row(pwi[1][..., t]),
                            jnp.zeros((G_A, 4, LANES), F32)], axis=1)
    return m.astype(BF16), p.astype(BF16), q.astype(BF16), coef


def _s5_kernel(u_ref, m_ref, p_ref, q_ref, coef_ref, y_ref, s_scr, x0_scr, *, n_chunks, bsz, rb):
    rows = n_chunks * bsz
    ncomp = 4
    for k in range(rows // rb):
        sl = slice(k * rb, (k + 1) * rb)
        s = jnp.dot(u_ref[sl, :], p_ref[...], preferred_element_type=F32)
        for c in range(ncomp):
            s_scr[c, sl, :] = s[:, c * LANES:(c + 1) * LANES]

    shp = (bsz, LANES)
    lfr = jnp.broadcast_to(coef_ref[0:1, :], shp)
    lfi = jnp.broadcast_to(coef_ref[1:2, :], shp)
    lbr = jnp.broadcast_to(coef_ref[2:3, :], shp)
    lbi = jnp.broadcast_to(coef_ref[3:4, :], shp)

    def step(j, carry):
        xfr, xfi, xbr, xbi = carry
        rf = pl.ds(j, bsz, stride=n_chunks)
        rbk = pl.ds(n_chunks - 1 - j, bsz, stride=n_chunks)
        x0_scr[0, rf, :] = xfr
        x0_scr[1, rf, :] = xfi
        x0_scr[2, rbk, :] = xbr
        x0_scr[3, rbk, :] = xbi
        nfr = lfr * xfr - lfi * xfi + s_scr[0, rf, :]
        nfi = lfr * xfi + lfi * xfr + s_scr[1, rf, :]
        nbr = lbr * xbr - lbi * xbi + s_scr[2, rbk, :]
        nbi = lbr * xbi + lbi * xbr + s_scr[3, rbk, :]
        return nfr, nfi, nbr, nbi

    z = jnp.zeros(shp, F32)
    lax.fori_loop(0, n_chunks, step, (z, z, z, z))

    for k in range(rows // rb):
        sl = slice(k * rb, (k + 1) * rb)
        x0 = jnp.concatenate([x0_scr[c, sl, :] for c in range(ncomp)], axis=1).astype(BF16)
        y = jnp.dot(u_ref[sl, :], m_ref[...], preferred_element_type=F32)
        y = y + jnp.dot(x0, q_ref[...], preferred_element_type=F32)
        y_ref[sl, :] = y.astype(y_ref.dtype)


def _s5_mixer(ut, weights, bsz, lp):
    m, p, q, coef = weights
    n_chunks = lp // S5_T
    rows = n_chunks * bsz
    rb = _row_tile(rows, rows, cap=640)
    return pl.pallas_call(
        functools.partial(_s5_kernel, n_chunks=n_chunks, bsz=bsz, rb=rb),
        grid=(G_A,),
        in_specs=[pl.BlockSpec((None, rows, S5_W), lambda g: (g, 0, 0)),
                  pl.BlockSpec((None, S5_W, S5_W), lambda g: (g, 0, 0)),
                  pl.BlockSpec((None, S5_W, 4 * LANES), lambda g: (g, 0, 0)),
                  pl.BlockSpec((None, 4 * LANES, S5_W), lambda g: (g, 0, 0)),
                  pl.BlockSpec((None, SUBLANES, LANES), lambda g: (g, 0, 0))],
        out_specs=pl.BlockSpec((None, rows, S5_W), lambda g: (g, 0, 0)),
        out_shape=jax.ShapeDtypeStruct((G_A, rows, S5_W), BF16),
        scratch_shapes=[pltpu.VMEM((4, rows, LANES), F32), pltpu.VMEM((4, rows, LANES), F32)],
        compiler_params=_cparams(1),
        name="s5_chunks",
    )(ut, m, p, q, coef)


def _lru_weights(conv_w, conv_b, w_r, b_r, w_i, b_i, lam):
    ncb = D_B // LRU_CB
    hpb = LRU_CB // BW_B

    def blockdiag(w):
        w = w.astype(F32).reshape(2, ncb, hpb, BW_B, BW_B)
        out = jnp.zeros((2, ncb, hpb, BW_B, hpb, BW_B), F32)
        for k in range(hpb):
            out = out.at[:, :, k, :, k, :].set(w[:, :, k])
        return out.reshape(2, ncb, LRU_CB, LRU_CB)
    wg = jnp.concatenate([blockdiag(w_r), blockdiag(w_i)], axis=-1).astype(BF16)
    bg = jnp.concatenate([b_r.astype(F32).reshape(2, ncb, 1, LRU_CB),
                          b_i.astype(F32).reshape(2, ncb, 1, LRU_CB)], axis=-1)
    cl = (-LRU_C * LOG2E * jax.nn.softplus(-lam.astype(F32))).reshape(2, ncb, 1, LRU_CB)
    cw = jnp.pad(conv_w.astype(F32), ((0, SUBLANES - CONV_B), (0, 0)))
    cb = conv_b.astype(F32).reshape(1, D_B)
    return cw, cb, wg, bg, cl


LRU_HALO = BF16_ROWS
N_CB = D_B // LRU_CB


def _lru_kernel(fp_ref, fc_ref, fn_ref, bp_ref, bc_ref, bn_ref, cw_ref, cb_ref, wg_ref, bg_ref, cl_ref,
                hf_ref, hb_ref, xw, a_scr, b_scr, h_scr, init_scr, carry_scr, *, tt, n_tiles):
    t = pl.program_id(1)
    seg = tt // SUBLANES
    sb = tt // 4
    halo = LRU_HALO
    dirs = ((fp_ref, fc_ref, fn_ref, t), (bp_ref, bc_ref, bn_ref, n_tiles - 1 - t))

    @pl.when(t == 0)
    def _():
        carry_scr[...] = jnp.zeros_like(carry_scr)

    for d, (p_ref, c_ref, n_ref, ti) in enumerate(dirs):
        keep = jnp.where(ti < n_tiles - 1, 1.0, 0.0)
        xw[d, 0:halo, :] = p_ref[...].astype(F32)
        xw[d, halo:halo + tt, :] = c_ref[...].astype(F32)
        xw[d, halo + tt:2 * halo + tt, :] = n_ref[...].astype(F32) * keep

    def gates(k, _):
        n = sb + 2 * SUBLANES
        base = pl.multiple_of(k * sb, SUBLANES)
        rows = pl.ds(base, sb)
        for d in range(2):
            for cb in range(N_CB):
                lanes = slice(cb * LRU_CB, (cb + 1) * LRU_CB)
                win = xw[d, pl.ds(base + halo - SUBLANES, n), lanes]
                lo, hi = SUBLANES, SUBLANES + sb
                xc = (cw_ref[0:1, lanes] * pltpu.roll(win, 2, 0)[lo:hi]
                      + cw_ref[1:2, lanes] * pltpu.roll(win, 1, 0)[lo:hi]
                      + cw_ref[2:3, lanes] * win[lo:hi]
                      + cw_ref[3:4, lanes] * pltpu.roll(win, n - 1, 0)[lo:hi]) + cb_ref[0:1, lanes]
                z = jnp.dot(xc.astype(BF16), wg_ref[d, cb], preferred_element_type=F32) + bg_ref[d, cb]
                r = _sigmoid(z[:, :LRU_CB])
                ig = _sigmoid(z[:, LRU_CB:])
                a = jnp.exp2(cl_ref[d, cb] * r)
                y = 1.0 - a * a
                root = jnp.where(y > 0.0, y * lax.rsqrt(y), 0.0)
                a_scr[d, cb, rows, :] = a
                b_scr[d, cb, rows, :] = root * (ig * xc)
        return 0
    lax.fori_loop(0, tt // sb, gates, 0)

    for k in range(-(-PAD // tt)):
        npad = min(tt, PAD - k * tt)

        @pl.when(t == k)
        def _(npad=npad):
            for cb in range(N_CB):
                b_scr[0, cb, 0:npad, :] = jnp.zeros((npad, LANES), F32)

    chains = [(d, cb) for d in range(2) for cb in range(N_CB)]

    def rows_of(d, i):
        return pl.ds(i if d == 0 else seg - 1 - i, SUBLANES, stride=seg)

    def p1(i, c):
        out = []
        for n, (d, cb) in enumerate(chains):
            h, acc = c[2 * n], c[2 * n + 1]
            a = a_scr[d, cb, rows_of(d, i), :]
            out += [a * h + b_scr[d, cb, rows_of(d, i), :], a * acc]
        return tuple(out)

    z8 = jnp.zeros((SUBLANES, LANES), F32)
    o8 = jnp.ones((SUBLANES, LANES), F32)
    ends = lax.fori_loop(0, seg, p1, (z8, o8) * len(chains))
    for n, (d, cb) in enumerate(chains):
        he, ae = ends[2 * n], ends[2 * n + 1]
        c = carry_scr[d, cb, 0:1, :]
        for s in (range(SUBLANES) if d == 0 else reversed(range(SUBLANES))):
            init_scr[d, cb, s:s + 1, :] = c
            c = he[s:s + 1] + ae[s:s + 1] * c
        carry_scr[d, cb, 0:1, :] = c

    def p2(i, c):
        out = []
        for n, (d, cb) in enumerate(chains):
            h = a_scr[d, cb, rows_of(d, i), :] * c[n] + b_scr[d, cb, rows_of(d, i), :]
            h_scr[d, cb, rows_of(d, i), :] = h
            out.append(h)
        return tuple(out)
    lax.fori_loop(0, seg, p2, tuple(init_scr[d, cb] for d, cb in chains))

    for d, o_ref in enumerate((hf_ref, hb_ref)):
        for cb in range(N_CB):
            o_ref[:, cb * LRU_CB:(cb + 1) * LRU_CB] = h_scr[d, cb].astype(o_ref.dtype)


def _lru_mixer(x_b, weights, bsz, lp):
    cw, cb, wg, bg, cl = weights
    tt = _lru_tile(lp)
    n_tiles = lp // tt
    hpt = tt // LRU_HALO
    last = lp // LRU_HALO - 1
    x3 = x_b.reshape(bsz, lp, D_B)

    def specs(tile_of):
        return [pl.BlockSpec((None, LRU_HALO, D_B), lambda b, t: (b, jnp.maximum(tile_of(t) * hpt - 1, 0), 0)),
                pl.BlockSpec((None, tt, D_B), lambda b, t: (b, tile_of(t), 0)),
                pl.BlockSpec((None, LRU_HALO, D_B), lambda b, t: (b, jnp.minimum((tile_of(t) + 1) * hpt, last), 0))]
    fwd = specs(lambda t: t)
    bwd = specs(lambda t: n_tiles - 1 - t)

    def whole(a):
        return pl.BlockSpec(a.shape, lambda b, t: (0,) * a.ndim)
    hf, hb = pl.pallas_call(
        functools.partial(_lru_kernel, tt=tt, n_tiles=n_tiles),
        grid=(bsz, n_tiles),
        in_specs=fwd + bwd + [whole(cw), whole(cb), whole(wg), whole(bg), whole(cl)],
        out_specs=[fwd[1], bwd[1]],
        out_shape=[jax.ShapeDtypeStruct((bsz, lp, D_B), BF16)] * 2,
        scratch_shapes=[pltpu.VMEM((2, tt + 2 * LRU_HALO, D_B), F32),
                        pltpu.VMEM((2, N_CB, tt, LANES), F32), pltpu.VMEM((2, N_CB, tt, LANES), F32),
                        pltpu.VMEM((2, N_CB, tt, LANES), F32),
                        pltpu.VMEM((2, N_CB, SUBLANES, LANES), F32),
                        pltpu.VMEM((2, N_CB, SUBLANES, LANES), F32)],
        compiler_params=_cparams(2),
        name="rglru",
    )(x3, x3, x3, x3, x3, x3, cw, cb, wg, bg, cl)
    return hf.reshape(bsz * lp, D_B), hb.reshape(bsz * lp, D_B)


def _ab_out_kernel(yt_ref, hf_ref, hb_ref, gb_ref, h_ref, permt_ref, wglu_ref, bglu_ref, wo_ref, o_ref, *, lp, tm):
    yb = ((hf_ref[...].astype(F32) + hb_ref[...].astype(F32)) * _gelu(gb_ref[...].astype(F32))).astype(BF16)
    units = []
    for q in range(tm // UNIT):
        def load(g, half, q=q):
            return yt_ref[g, q * BF16_ROWS:(q + 1) * BF16_ROWS, half * LANES:(half + 1) * LANES].astype(F32)
        ysc = _chunks_to_tokens(load).astype(BF16)
        units.append(jnp.dot(permt_ref[...], ysc, preferred_element_type=F32))
    ya = _gelu(jnp.concatenate(units, axis=0))
    gate = _sigmoid(jnp.dot(ya.astype(BF16), wglu_ref[...], preferred_element_type=F32) + bglu_ref[...])
    ya = (ya * gate).astype(BF16)
    acc = jnp.dot(ya, wo_ref[0:D_A, :], preferred_element_type=F32)
    acc = acc + jnp.dot(yb, wo_ref[D_A:, :], preferred_element_type=F32)
    valid = _valid_rows(pl.program_id(0), tm, lp, acc.shape)
    o_ref[...] = jnp.where(valid, h_ref[...] + acc, 0.0)


def _ab_out(yt, hf, hb, g_b, h2, w_glu, b_glu, w_out, lp):
    rows, d = h2.shape
    tm = _row_tile(rows, lp, mult=UNIT)
    return pl.pallas_call(
        functools.partial(_ab_out_kernel, lp=lp, tm=tm),
        grid=(rows // tm,),
        in_specs=[pl.BlockSpec((G_A, tm // S5_T, S5_W), lambda i: (0, i, 0)),
                  pl.BlockSpec((tm, D_B), lambda i: (i, 0)),
                  pl.BlockSpec((tm, D_B), lambda i: (i, 0)),
                  pl.BlockSpec((tm, D_B), lambda i: (i, 0)),
                  pl.BlockSpec((tm, d), lambda i: (i, 0)),
                  pl.BlockSpec((UNIT, UNIT), lambda i: (0, 0)),
                  pl.BlockSpec((D_A, D_A), lambda i: (0, 0)),
                  pl.BlockSpec((1, D_A), lambda i: (0, 0)),
                  pl.BlockSpec((D_A + D_B, d), lambda i: (0, 0))],
        out_specs=pl.BlockSpec((tm, d), lambda i: (i, 0)),
        out_shape=jax.ShapeDtypeStruct((rows, d), F32),
        compiler_params=_cparams(1),
        name="ab_out",
    )(yt, hf, hb, g_b, h2, _unit_perm().T, w_glu.astype(BF16), b_glu.astype(F32).reshape(1, D_A),
      w_out.astype(BF16))


def _proj_res_kernel(x_ref, h_ref, w_ref, o_ref, *, lp, tm):
    acc = jnp.dot(x_ref[...], w_ref[...], preferred_element_type=F32)
    valid = _valid_rows(pl.program_id(0), tm, lp, acc.shape)
    o_ref[...] = jnp.where(valid, h_ref[...] + acc, 0.0)


def _proj_res(x, h2, w, lp):
    rows, d = h2.shape
    tm = _row_tile(rows, lp)
    k = x.shape[1]
    return pl.pallas_call(
        functools.partial(_proj_res_kernel, lp=lp, tm=tm),
        grid=(rows // tm,),
        in_specs=[pl.BlockSpec((tm, k), lambda i: (i, 0)),
                  pl.BlockSpec((tm, d), lambda i: (i, 0)),
                  pl.BlockSpec((k, d), lambda i: (0, 0))],
        out_specs=pl.BlockSpec((tm, d), lambda i: (i, 0)),
        out_shape=jax.ShapeDtypeStruct((rows, d), F32),
        compiler_params=_cparams(1),
        name="proj_res",
    )(x, h2, w.astype(BF16))


LOG2E = math.log2(math.e)
N_PAIRS = N_Q_HEADS // 2


def _pair_heads():
    return [((2 * hp) * GQ + g, (2 * hp + 1) * GQ + g) for hp in range(N_KV_HEADS // 2) for g in range(GQ)]


def _attn_bias(lp):
    nb = lp // BLOCK
    assert nb >= 3
    qi = jnp.arange(BLOCK)[:, None]
    ki = jnp.arange(3 * BLOCK)[None, :]
    dist = jnp.abs(qi + BLOCK - ki)
    slopes = 2.0 ** (-8.0 * jnp.arange(1, N_Q_HEADS + 1, dtype=F32) / N_Q_HEADS)
    tables = []
    for n in (0, 1, 2 if nb > 3 else None, nb - 1):
        if n is None:
            tables.append(tables[-1])
            continue
        key_pos = (n - 1) * BLOCK + ki
        ok = (dist <= WINDOW) & (key_pos >= PAD) & (key_pos < lp)
        alibi = -slopes[:, None, None] * dist.astype(F32)[None] * LOG2E
        tables.append(jnp.where(ok[None], alibi, NEG))
    return jnp.stack(tables)


def _attn_kernel(q_ref, kp_ref, kc_ref, kn_ref, vp_ref, vc_ref, vn_ref, bias_ref, sink_ref, o_ref,
                 k_scr, v_scr, s_scr, m_scr, p_scr):
    low_kv = lax.broadcasted_iota(jnp.int32, (3 * BLOCK, LANES), 1) < HEAD_DIM
    low_q = lax.broadcasted_iota(jnp.int32, (BLOCK, LANES), 1) < HEAD_DIM
    heads = _pair_heads()
    n_hp = N_KV_HEADS // 2
    for hp in range(n_hp):
        cols = slice(hp * LANES, (hp + 1) * LANES)
        kpair = jnp.concatenate([kp_ref[:, cols], kc_ref[:, cols], kn_ref[:, cols]], axis=0)
        vpair = jnp.concatenate([vp_ref[:, cols], vc_ref[:, cols], vn_ref[:, cols]], axis=0)
        zero = jnp.zeros_like(kpair)
        one = jnp.ones_like(vpair)
        k_scr[hp, 0] = jnp.where(low_kv, kpair, zero)
        k_scr[hp, 1] = jnp.where(low_kv, zero, kpair)
        v_scr[hp, 0] = jnp.where(low_kv, vpair, one)
        v_scr[hp, 1] = jnp.where(low_kv, one, vpair)
    tiles = [(hp, g, part) for hp in range(n_hp) for g in range(GQ) for part in range(2)]
    for t, (hp, g, part) in enumerate(tiles):
        pair = hp * GQ + g
        qp = q_ref[:, pair * LANES:(pair + 1) * LANES]
        s = lax.dot_general(qp, k_scr[hp, part], (((1,), (1,)), ((), ())), preferred_element_type=F32)
        s_scr[t] = s + bias_ref[heads[pair][part]]
    for t, (hp, g, part) in enumerate(tiles):
        sink = sink_ref[0, heads[hp * GQ + g][part]]
        m = jnp.maximum(jnp.max(s_scr[t], axis=-1, keepdims=True), sink)
        m_scr[t] = jnp.broadcast_to(m, (BLOCK, LANES))
    for t in range(len(tiles)):
        m = m_scr[t]
        for c in range(3):
            cs = slice(c * BLOCK, (c + 1) * BLOCK)
            p_scr[t, :, cs] = jnp.exp2(s_scr[t, :, cs] - m).astype(BF16)
    for t, (hp, g, part) in enumerate(tiles):
        pair = hp * GQ + g
        sink = sink_ref[0, heads[pair][part]]
        oe = jnp.dot(p_scr[t], v_scr[hp, part], preferred_element_type=F32)
        o = oe / (pltpu.roll(oe, HEAD_DIM, 1) + jnp.exp2(sink - m_scr[t]))
        if part == 0:
            first = o
        else:
            o_ref[:, pair * LANES:(pair + 1) * LANES] = jnp.where(low_q, first, o).astype(o_ref.dtype)


def _attention(q, k, v, sink, bsz, lp):
    nb = lp // BLOCK
    dq = N_Q_HEADS * HEAD_DIM
    dkv = N_KV_HEADS * HEAD_DIM
    q3, k3, v3 = q.reshape(bsz, lp, dq), k.reshape(bsz, lp, dkv), v.reshape(bsz, lp, dkv)
    sink2 = (sink.astype(F32) * LOG2E).reshape(1, N_Q_HEADS)
    kv_prev = pl.BlockSpec((None, BLOCK, dkv), lambda b, n: (b, jnp.maximum(n - 1, 0), 0))
    kv_cur = pl.BlockSpec((None, BLOCK, dkv), lambda b, n: (b, n, 0))
    kv_next = pl.BlockSpec((None, BLOCK, dkv), lambda b, n: (b, jnp.minimum(n + 1, nb - 1), 0))

    def bias_case(b, n):
        return (jnp.where(n == 0, 0, jnp.where(n == 1, 1, jnp.where(n == nb - 1, 3, 2))), 0, 0, 0)
    out = pl.pallas_call(
        _attn_kernel,
        grid=(bsz, nb),
        in_specs=[pl.BlockSpec((None, BLOCK, dq), lambda b, n: (b, n, 0)),
                  kv_prev, kv_cur, kv_next, kv_prev, kv_cur, kv_next,
                  pl.BlockSpec((None, N_Q_HEADS, BLOCK, 3 * BLOCK), bias_case),
                  pl.BlockSpec(memory_space=pltpu.SMEM)],
        out_specs=pl.BlockSpec((None, BLOCK, dq), lambda b, n: (b, n, 0)),
        out_shape=jax.ShapeDtypeStruct((bsz, lp, dq), BF16),
        scratch_shapes=[pltpu.VMEM((N_KV_HEADS // 2, 2, 3 * BLOCK, LANES), BF16),
                        pltpu.VMEM((N_KV_HEADS // 2, 2, 3 * BLOCK, LANES), BF16),
                        pltpu.VMEM((N_Q_HEADS, BLOCK, 3 * BLOCK), F32),
                        pltpu.VMEM((N_Q_HEADS, BLOCK, LANES), F32),
                        pltpu.VMEM((N_Q_HEADS, BLOCK, 3 * BLOCK), BF16)],
        compiler_params=_cparams(2),
        name="swa",
    )(q3, k3, k3, k3, v3, v3, v3, _attn_bias(lp), sink2)
    return out.reshape(bsz * lp, dq)


def _ffn_kernel(*refs, lp, tm, n_tiles, final, n_cur):
    hp_ref, hc_refs = refs[0], refs[1:1 + n_cur]
    hn_ref, g_ref, wup_ref, cw_ref, cb_ref, wdn_ref, gf_ref, o_ref, nat_scr, hx_scr, act_scr = refs[1 + n_cur:]
    i = pl.program_id(1 if final else 0)
    g = g_ref[...]
    hc = jnp.concatenate([r[...] for r in hc_refs], axis=0) if n_cur > 1 else hc_refs[0][...]
    n = tm + 2 * FFN_HALO
    nv = n // SUBLANES
    nslab = D_MODEL // LANES

    def put(r0, x):
        for l in range(nslab):
            nat_scr[l, r0:r0 + x.shape[0], :] = x[:, l * LANES:(l + 1) * LANES]
    keep = jnp.where(i < n_tiles - 1, 1.0, 0.0)
    put(0, _rmsnorm(hp_ref[...], g))
    put(FFN_HALO, _rmsnorm(hc, g))
    put(FFN_HALO + tm, _rmsnorm(hn_ref[...], g) * keep)

    def to_strided(jp, _):
        r = pl.ds(pl.multiple_of(jp * BF16_ROWS, BF16_ROWS), BF16_ROWS)
        for l in range(nslab):
            two = jnp.concatenate([nat_scr[l, pl.ds(2 * jp, SUBLANES, stride=nv), :],
                                   nat_scr[l, pl.ds(2 * jp + 1, SUBLANES, stride=nv), :]], axis=0)
            hx_scr[r, l * LANES:(l + 1) * LANES] = two.astype(BF16)
        return 0
    lax.fori_loop(0, nv // 2, to_strided, 0)

    def conv(u, cols):
        prev = jnp.concatenate([pltpu.roll(u[n - SUBLANES:], 1, 0), u[:n - SUBLANES]], axis=0)
        nxt = jnp.concatenate([u[SUBLANES:], pltpu.roll(u[:SUBLANES], SUBLANES - 1, 0)], axis=0)
        return (cw_ref[0:1, cols] * prev + cw_ref[1:2, cols] * u + cw_ref[2:3, cols] * nxt) + cb_ref[0:1, cols]

    for c in range(D_FF // FFN_TF):
        ca = slice(c * FFN_TF, (c + 1) * FFN_TF)
        cg = slice(D_FF + c * FFN_TF, D_FF + (c + 1) * FFN_TF)
        ua = jnp.dot(hx_scr[...], wup_ref[:, ca], preferred_element_type=F32)
        ug = jnp.dot(hx_scr[...], wup_ref[:, cg], preferred_element_type=F32)
        act_scr[:, ca] = (_gelu(conv(ug, cg)) * conv(ua, ca)).astype(BF16)

    ys = jnp.dot(act_scr[...], wdn_ref[...], preferred_element_type=F32)
    for j in range(nv):
        for l in range(nslab):
            nat_scr[l, pl.ds(j, SUBLANES, stride=nv), :] = ys[j * SUBLANES:(j + 1) * SUBLANES,
                                                              l * LANES:(l + 1) * LANES]
    y = hc + jnp.concatenate([nat_scr[l, FFN_HALO:FFN_HALO + tm, :] for l in range(nslab)], axis=1)
    if final:
        o_ref[...] = _rmsnorm(y, gf_ref[...])
    else:
        o_ref[...] = jnp.where(_valid_rows(i, tm, lp, y.shape), y, 0.0)


def _conv_ffn(h2, g, w_up, conv_w, conv_b, w_down, g_final, bsz, lp, final):
    rows, d = h2.shape
    const = dict(pipeline_mode=pl.Buffered(1))
    cw = jnp.pad(conv_w.astype(F32), ((0, SUBLANES - CONV_F), (0, 0)))
    if final:
        seq = lp - FRONT
        tm = _row_tile(seq, seq, mult=BLOCK)
        n_tiles, n_cur = seq // tm, tm // BLOCK
        grid = (bsz, n_tiles)
        last = lp // FFN_HALO - 1
        h_in = h2.reshape(bsz, lp, d)
        prev = pl.BlockSpec((None, FFN_HALO, d), lambda b, j: (b, (FRONT + j * tm) // FFN_HALO - 1, 0))
        cur = [pl.BlockSpec((None, BLOCK, d), lambda b, j, k=k: (b, 1 + j * n_cur + k, 0)) for k in range(n_cur)]
        nxt = pl.BlockSpec((None, FFN_HALO, d),
                           lambda b, j: (b, jnp.minimum((FRONT + (j + 1) * tm) // FFN_HALO, last), 0))
        out_spec = pl.BlockSpec((None, tm, d), lambda b, j: (b, j, 0))
        out_shape = jax.ShapeDtypeStruct((bsz, seq, d), F32)

        def whole(shape):
            return pl.BlockSpec(shape, lambda b, j: (0,) * len(shape))
    else:
        tm = _row_tile(rows, lp)
        n_tiles, n_cur = rows // tm, 1
        grid = (n_tiles,)
        hb = tm // FFN_HALO
        last = rows // FFN_HALO - 1
        h_in = h2
        prev = pl.BlockSpec((FFN_HALO, d), lambda i: (jnp.maximum(i * hb - 1, 0), 0))
        cur = [pl.BlockSpec((tm, d), lambda i: (i, 0))]
        nxt = pl.BlockSpec((FFN_HALO, d), lambda i: (jnp.minimum((i + 1) * hb, last), 0))
        out_spec = pl.BlockSpec((tm, d), lambda i: (i, 0))
        out_shape = jax.ShapeDtypeStruct((rows, d), F32)

        def whole(shape):
            return pl.BlockSpec(shape, lambda i: (0,) * len(shape))
    weights = [pl.BlockSpec((1, d), whole((1, d)).index_map),
               pl.BlockSpec((d, 2 * D_FF), whole((d, 2 * D_FF)).index_map, **const),
               whole((SUBLANES, 2 * D_FF)), whole((1, 2 * D_FF)),
               pl.BlockSpec((D_FF, d), whole((D_FF, d)).index_map, **const),
               whole((1, d))]
    return pl.pallas_call(
        functools.partial(_ffn_kernel, lp=lp, tm=tm, n_tiles=n_tiles, final=final, n_cur=n_cur),
        grid=grid,
        in_specs=[prev] + cur + [nxt] + weights,
        out_specs=out_spec,
        out_shape=out_shape,
        scratch_shapes=[pltpu.VMEM((d // LANES, tm + 2 * FFN_HALO, LANES), F32),
                        pltpu.VMEM((tm + 2 * FFN_HALO, d), BF16),
                        pltpu.VMEM((tm + 2 * FFN_HALO, D_FF), BF16)],
        compiler_params=_cparams(len(grid)),
        name="conv_ffn",
    )(*([h_in] * (2 + n_cur)), g.astype(F32).reshape(1, d), w_up.astype(BF16), cw,
      conv_b.astype(F32).reshape(1, 2 * D_FF), w_down.astype(BF16), g_final.astype(F32).reshape(1, d))


def _trunk(x, prm):
    bsz, seq, d = x.shape
    lp = seq + FRONT
    assert seq % BLOCK == 0 and d == D_MODEL

    h, u_a, x_b, g_b = _in_proj(x, prm['meta_tokens'], prm['norm_mix_g'][0], prm['w_in_ab'][0].astype(BF16), lp)
    s5w = _s5_weights(prm['s5_lambda_re'][0], prm['s5_lambda_im'][0], prm['s5_log_dt'][0],
                      prm['s5_b_re'][0], prm['s5_b_im'][0], prm['s5_c_re'][0], prm['s5_c_im'][0], prm['s5_d'][0])
    y_a = _s5_mixer(u_a, s5w, bsz, lp)
    lruw = _lru_weights(prm['lru_conv_w'][0], prm['lru_conv_b'][0], prm['lru_w_r'][0], prm['lru_b_r'][0],
                        prm['lru_w_i'][0], prm['lru_b_i'][0], prm['lru_lambda'][0])
    hf, hb = _lru_mixer(x_b, lruw, bsz, lp)
    h = _ab_out(y_a, hf, hb, g_b, h, prm['w_glu'][0], prm['b_glu'][0], prm['w_out_ab'][0], lp)
    h = _conv_ffn(h, prm['norm_ffn_g'][0], prm['w_up'][0], prm['ffn_conv_w'][0], prm['ffn_conv_b'][0],
                  prm['w_down'][0], prm['final_norm_g'], bsz, lp, final=False)

    dq, dkv = N_Q_HEADS * HEAD_DIM, N_KV_HEADS * HEAD_DIM
    head_order = jnp.asarray([h_ for pair in _pair_heads() for h_ in pair])
    col_order = (head_order[:, None] * HEAD_DIM + jnp.arange(HEAD_DIM)[None, :]).reshape(-1)
    w_qkv = prm['w_qkv'][0]
    w_qkv = jnp.concatenate([w_qkv[:, :dq][:, col_order], w_qkv[:, dq:]], axis=1).astype(BF16)
    q, k, v = _norm_mm(h, prm['norm_mix_g'][1], w_qkv, (dq, dkv, dkv),
                       (LOG2E / math.sqrt(HEAD_DIM), 1.0, 1.0), lp)
    o = _attention(q, k, v, prm['attn_sink'][0], bsz, lp)
    h = _proj_res(o, h, prm['w_o'][0][col_order], lp)
    return _conv_ffn(h, prm['norm_ffn_g'][1], prm['w_up'][1], prm['ffn_conv_w'][1], prm['ffn_conv_b'][1],
                     prm['w_down'][1], prm['final_norm_g'], bsz, lp, final=True)


def kernel(x_prompt, x_sample, meta_tokens, norm_mix_g, norm_ffn_g, final_norm_g, w_in_ab, s5_lambda_re, s5_lambda_im, s5_log_dt, s5_b_re, s5_b_im, s5_c_re, s5_c_im, s5_d, w_glu, b_glu, lru_conv_w, lru_conv_b, lru_w_r, lru_b_r, lru_w_i, lru_b_i, lru_lambda, w_out_ab, w_qkv, w_o, attn_sink, w_up, ffn_conv_w, ffn_conv_b, w_down):
    prm = dict(meta_tokens=meta_tokens, norm_mix_g=norm_mix_g, norm_ffn_g=norm_ffn_g, final_norm_g=final_norm_g,
               w_in_ab=w_in_ab, s5_lambda_re=s5_lambda_re, s5_lambda_im=s5_lambda_im, s5_log_dt=s5_log_dt,
               s5_b_re=s5_b_re, s5_b_im=s5_b_im, s5_c_re=s5_c_re, s5_c_im=s5_c_im, s5_d=s5_d, w_glu=w_glu,
               b_glu=b_glu, lru_conv_w=lru_conv_w, lru_conv_b=lru_conv_b, lru_w_r=lru_w_r, lru_b_r=lru_b_r,
               lru_w_i=lru_w_i, lru_b_i=lru_b_i, lru_lambda=lru_lambda, w_out_ab=w_out_ab, w_qkv=w_qkv,
               w_o=w_o, attn_sink=attn_sink, w_up=w_up, ffn_conv_w=ffn_conv_w, ffn_conv_b=ffn_conv_b,
               w_down=w_down)
    return (_trunk(x_prompt, prm), _trunk(x_sample, prm))
```

```python
import functools
import math

import jax
import jax.numpy as jnp
from jax import lax
from jax.experimental import pallas as pl
from jax.experimental.pallas import tpu as pltpu

D_MODEL = 1024
N_META = 16
D_A = 512
S5_GROUP = 16
G_A = D_A // S5_GROUP
S5_STATE = 64
D_B = 512
H_B = 8
BW_B = D_B // H_B
LRU_C = 8.0
CONV_B = 4
CONV_B_LEFT = 2
HEAD_DIM = 64
N_Q_HEADS = 16
N_KV_HEADS = 4
GQ = N_Q_HEADS // N_KV_HEADS
WINDOW = 128
BLOCK = 128
D_FF = 2816
CONV_F = 3
EPS = 1e-6
NEG = -1e30

PAD = BLOCK - N_META
FRONT = PAD + N_META
S5_T = 16
S5_W = S5_T * S5_GROUP
LANES = 128
SUBLANES = 8
BF16_ROWS = 16
FFN_HALO = BF16_ROWS
FFN_TF = 256
FFN_TM = 1024
LRU_CB = LANES
VMEM_LIMIT = 56 * 1024 * 1024

F32 = jnp.float32
BF16 = jnp.bfloat16


def _cparams(n_axes):
    return pltpu.CompilerParams(
        dimension_semantics=("arbitrary",) * n_axes, vmem_limit_bytes=VMEM_LIMIT)


def _row_tile(rows, lp, cap=512, mult=BF16_ROWS):
    best = None
    t = mult
    while t <= min(cap, lp):
        if rows % t == 0:
            best = t
        t += mult
    assert best is not None, (rows, lp, cap, mult)
    return best


def _lru_tile(lp):
    q = lp // 32
    assert lp % 32 == 0
    odd = q
    while odd % 2 == 0:
        odd //= 2
    best = 1
    for d in range(1, odd + 1, 2):
        if odd % d == 0 and 32 * d <= 640:
            best = d
    return 32 * best


def _rmsnorm(x, g):
    ms = jnp.mean(x * x, axis=-1, keepdims=True)
    return x * lax.rsqrt(ms + EPS) * g


def _gelu(x):
    k1 = -2.0 * math.sqrt(2.0 / math.pi) * math.log2(math.e)
    return x / (1.0 + jnp.exp2(x * (k1 + (k1 * 0.044715) * (x * x))))


def _sigmoid(x):
    return 1.0 / (1.0 + jnp.exp(-x))


def _valid_rows(tile_idx, tm, lp, shape):
    pos = lax.rem(tile_idx * tm, lp) + lax.broadcasted_iota(jnp.int32, shape, 0)
    pos = jnp.where(pos >= lp, pos - lp, pos)
    return pos >= PAD


def _norm_mm_kernel(h_ref, g_ref, w_ref, *o_refs, scales):
    hn = _rmsnorm(h_ref[...], g_ref[...]).astype(BF16)
    off = 0
    for o_ref, sc in zip(o_refs, scales):
        n = o_ref.shape[-1]
        z = jnp.dot(hn, w_ref[:, off:off + n], preferred_element_type=F32)
        if sc != 1.0:
            z = z * sc
        o_ref[...] = z.astype(o_ref.dtype)
        off += n


def _norm_mm(h2, g, w, splits, scales, lp):
    rows, d = h2.shape
    tm = _row_tile(rows, lp)
    n = w.shape[1]
    return pl.pallas_call(
        functools.partial(_norm_mm_kernel, scales=scales),
        grid=(rows // tm,),
        in_specs=[pl.BlockSpec((tm, d), lambda i: (i, 0)),
                  pl.BlockSpec((1, d), lambda i: (0, 0)),
                  pl.BlockSpec((d, n), lambda i: (0, 0))],
        out_specs=[pl.BlockSpec((tm, s), lambda i: (i, 0)) for s in splits],
        out_shape=[jax.ShapeDtypeStruct((rows, s), BF16) for s in splits],
        compiler_params=_cparams(1),
        name="norm_mm",
    )(h2, g.reshape(1, d), w)


UNIT = S5_T * BF16_ROWS
GRP_PER_VREG = LANES // S5_GROUP


def _unit_perm():
    r = jnp.arange(UNIT)
    src = (r % BF16_ROWS) * S5_T + r // BF16_ROWS
    return (src[:, None] == jnp.arange(UNIT)[None, :]).astype(BF16)


def _tokens_to_chunks(zp, store):
    grp = lax.broadcasted_iota(jnp.int32, (BF16_ROWS, LANES), 1) // S5_GROUP
    for j in range(D_A // LANES):
        v = [zp[s * BF16_ROWS:(s + 1) * BF16_ROWS, j * LANES:(j + 1) * LANES] for s in range(S5_T)]
        for gl in range(GRP_PER_VREG):
            for half in range(S5_T // GRP_PER_VREG):
                acc = None
                for p in range(GRP_PER_VREG):
                    src = v[half * GRP_PER_VREG + p]
                    shift = S5_GROUP * ((p - gl) % GRP_PER_VREG)
                    val = src if shift == 0 else pltpu.roll(src, shift, 1)
                    acc = val if acc is None else jnp.where(grp == p, val, acc)
                store(j * GRP_PER_VREG + gl, half, acc)


def _chunks_to_tokens(load):
    grp = lax.broadcasted_iota(jnp.int32, (BF16_ROWS, LANES), 1) // S5_GROUP
    cols = []
    for j in range(D_A // LANES):
        rows = []
        for half in range(S5_T // GRP_PER_VREG):
            y = [load(j * GRP_PER_VREG + gl, half) for gl in range(GRP_PER_VREG)]
            for p in range(GRP_PER_VREG):
                acc = None
                for gl in range(GRP_PER_VREG):
                    shift = S5_GROUP * ((gl - p) % GRP_PER_VREG)
                    val = y[gl] if shift == 0 else pltpu.roll(y[gl], shift, 1)
                    acc = val if acc is None else jnp.where(grp == gl, val, acc)
                rows.append(acc)
        cols.append(jnp.concatenate(rows, axis=0))
    return jnp.concatenate(cols, axis=1)


def _padded_rows(x_refs, meta_ref, nb):
    first = pl.program_id(0) * len(x_refs)
    blocks = [jnp.where(lax.rem(first + k, nb) == 0, meta_ref[...], r[...]) for k, r in enumerate(x_refs)]
    return jnp.concatenate(blocks, axis=0)


def _padded_row_specs(x, lp, tm):
    nb = lp // BLOCK
    n_cur = tm // BLOCK
    d = x.shape[-1]
    specs = [pl.BlockSpec((None, BLOCK, d),
                          lambda i, k=k: ((i * n_cur + k) // nb, jnp.maximum((i * n_cur + k) % nb - 1, 0), 0))
             for k in range(n_cur)]
    return specs + [pl.BlockSpec((BLOCK, d), lambda i: (0, 0))], [x] * n_cur


def _in_proj_kernel(*refs, tm, nb):
    n_cur = tm // BLOCK
    x_refs = refs[:n_cur]
    meta_ref, g_ref, w_ref, perm_ref, u_ref, xb_ref, gb_ref = refs[n_cur:]
    hn = _rmsnorm(_padded_rows(x_refs, meta_ref, nb), g_ref[...]).astype(BF16)
    xb_ref[...] = jnp.dot(hn, w_ref[:, D_A:D_A + D_B], preferred_element_type=F32).astype(BF16)
    gb_ref[...] = jnp.dot(hn, w_ref[:, D_A + D_B:], preferred_element_type=F32).astype(BF16)
    za = jnp.dot(hn, w_ref[:, 0:D_A], preferred_element_type=F32).astype(BF16)
    for q in range(tm // UNIT):
        zp = jnp.dot(perm_ref[...], za[q * UNIT:(q + 1) * UNIT], preferred_element_type=F32)

        def store(g, half, x, q=q):
            u_ref[g, q * BF16_ROWS:(q + 1) * BF16_ROWS, half * LANES:(half + 1) * LANES] = x.astype(BF16)
        _tokens_to_chunks(zp, store)


def _in_proj(x, meta_blk, g, w, lp):
    bsz, seq, d = x.shape
    rows = bsz * lp
    tm = _row_tile(rows, lp, mult=UNIT)
    n = w.shape[1]
    x_specs, x_ops = _padded_row_specs(x, lp, tm)
    return pl.pallas_call(
        functools.partial(_in_proj_kernel, tm=tm, nb=lp // BLOCK),
        grid=(rows // tm,),
        in_specs=x_specs + [pl.BlockSpec((1, d), lambda i: (0, 0)),
                            pl.BlockSpec((d, n), lambda i: (0, 0)),
                            pl.BlockSpec((UNIT, UNIT), lambda i: (0, 0))],
        out_specs=[pl.BlockSpec((G_A, tm // S5_T, S5_W), lambda i: (0, i, 0)),
                   pl.BlockSpec((tm, D_B), lambda i: (i, 0)),
                   pl.BlockSpec((tm, D_B), lambda i: (i, 0))],
        out_shape=[jax.ShapeDtypeStruct((G_A, rows // S5_T, S5_W), BF16),
                   jax.ShapeDtypeStruct((rows, D_B), BF16),
                   jax.ShapeDtypeStruct((rows, D_B), BF16)],
        compiler_params=_cparams(1),
        name="in_proj",
    )(*x_ops, meta_blk, g.reshape(1, d), w, _unit_perm())


def _cmul(ar, ai, br, bi):
    return ar * br - ai * bi, ar * bi + ai * br


def _s5_weights(lam_re, lam_im, log_dt, b_re, b_im, c_re, c_im, d_skip):
    hp = lax.Precision.HIGHEST
    t = S5_T
    lr = jnp.minimum(lam_re.astype(F32), -1e-4)
    li = lam_im.astype(F32)
    dt = jnp.exp(log_dt.astype(F32))[..., None]
    mag = jnp.exp(lr * dt)
    lbr, lbi = mag * jnp.cos(li * dt), mag * jnp.sin(li * dt)
    den = lr * lr + li * li
    xr, xi = lbr - 1.0, lbi
    fr, fi = (xr * lr + xi * li) / den, (xi * lr - xr * li) / den
    bbr, bbi = _cmul(fr[..., None], fi[..., None], b_re.astype(F32), b_im.astype(F32))
    pr, pi = [jnp.ones_like(lbr)], [jnp.zeros_like(lbi)]
    for _ in range(t):
        nr, ni = _cmul(pr[-1], pi[-1], lbr, lbi)
        pr.append(nr)
        pi.append(ni)
    pwr, pwi = jnp.stack(pr, -1), jnp.stack(pi, -1)
    cr, ci = c_re.astype(F32), c_im.astype(F32)
    wr, wi = _cmul(cr[..., None], ci[..., None], pwr[:, :, None], pwi[:, :, None])
    kk = (jnp.einsum('dgonk,dgni->dgkio', wr, bbr, precision=hp)
          - jnp.einsum('dgonk,dgni->dgkio', wi, bbi, precision=hp))[:, :, :t]
    eye = jnp.eye(S5_GROUP, dtype=F32) * d_skip.astype(F32).reshape(G_A, 1, S5_GROUP)
    k_all = jnp.concatenate([kk[1][:, 1:][:, ::-1],
                             (kk[0][:, 0] + kk[1][:, 0] + eye)[:, None],
                             kk[0][:, 1:]], axis=1)
    sidx = jnp.arange(t)
    lag = sidx[None, :] - sidx[:, None] + (t - 1)
    m = k_all[:, lag]
    m = m.transpose(0, 1, 3, 2, 4).reshape(G_A, S5_W, S5_W)
    pfr, pfi = _cmul(pwr[0][..., :t][..., ::-1][..., None], pwi[0][..., :t][..., ::-1][..., None],
                     bbr[0][:, :, None, :], bbi[0][:, :, None, :])
    pbr, pbi = _cmul(pwr[1][..., :t][..., None], pwi[1][..., :t][..., None],
                     bbr[1][:, :, None, :], bbi[1][:, :, None, :])

    def p_cols(x):
        x = x.transpose(0, 2, 3, 1).reshape(G_A, S5_W, S5_STATE)
        return jnp.pad(x, ((0, 0), (0, 0), (0, LANES - S5_STATE)))
    p = jnp.concatenate([p_cols(pfr), p_cols(pfi), p_cols(pbr), p_cols(pbi)], axis=-1)

    def q_rows(x):
        x = x.transpose(0, 2, 3, 1).reshape(G_A, S5_STATE, S5_W)
        return jnp.pad(x, ((0, 0), (0, LANES - S5_STATE), (0, 0)))
    q = jnp.concatenate([q_rows(wr[0][..., 1:]), q_rows(-wi[0][..., 1:]),
                         q_rows(wr[1][..., 1:][..., ::-1]), q_rows(-wi[1][..., 1:][..., ::-1])], axis=1)

    def c_row(x):
        return jnp.pad(x, ((0, 0), (0, LANES - S5_STATE)))[:, None]
    coef = jnp.concatenate([c_row(pwr[0][..., t]), c_row(pwi[0][..., t]),
                            c_row(pwr[1][..., t]), c_row(pwi[1][..., t]),
                            jnp.zeros((G_A, 4, LANES), F32)], axis=1)
    return m.astype(BF16), p.astype(BF16), q.astype(BF16), coef


def _s5_kernel(u_ref, m_ref, p_ref, q_ref, coef_ref, y_ref, s_scr, x0_scr, *, n_chunks, bsz, rb):
    rows = n_chunks * bsz
    ncomp = 4
    for k in range(rows // rb):
        sl = slice(k * rb, (k + 1) * rb)
        s = jnp.dot(u_ref[sl, :], p_ref[...], preferred_element_type=F32)
        for c in range(ncomp):
            s_scr[c, sl, :] = s[:, c * LANES:(c + 1) * LANES]

    shp = (bsz, LANES)
    lfr = jnp.broadcast_to(coef_ref[0:1, :], shp)
    lfi = jnp.broadcast_to(coef_ref[1:2, :], shp)
    lbr = jnp.broadcast_to(coef_ref[2:3, :], shp)
    lbi = jnp.broadcast_to(coef_ref[3:4, :], shp)

    def step(j, carry):
        xfr, xfi, xbr, xbi = carry
        rf = pl.ds(j, bsz, stride=n_chunks)
        rbk = pl.ds(n_chunks - 1 - j, bsz, stride=n_chunks)
        x0_scr[0, rf, :] = xfr
        x0_scr[1, rf, :] = xfi
        x0_scr[2, rbk, :] = xbr
        x0_scr[3, rbk, :] = xbi
        nfr = lfr * xfr - lfi * xfi + s_scr[0, rf, :]
        nfi = lfr * xfi + lfi * xfr + s_scr[1, rf, :]
        nbr = lbr * xbr - lbi * xbi + s_scr[2, rbk, :]
        nbi = lbr * xbi + lbi * xbr + s_scr[3, rbk, :]
        return nfr, nfi, nbr, nbi

    z = jnp.zeros(shp, F32)
    lax.fori_loop(0, n_chunks, step, (z, z, z, z), unroll=4 if n_chunks % 4 == 0 else 1)

    for k in range(rows // rb):
        sl = slice(k * rb, (k + 1) * rb)
        x0 = jnp.concatenate([x0_scr[c, sl, :] for c in range(ncomp)], axis=1).astype(BF16)
        y = jnp.dot(u_ref[sl, :], m_ref[...], preferred_element_type=F32)
        y = y + jnp.dot(x0, q_ref[...], preferred_element_type=F32)
        y_ref[sl, :] = y.astype(y_ref.dtype)


def _s5_mixer(ut, weights, bsz, lp):
    m, p, q, coef = weights
    n_chunks = lp // S5_T
    rows = n_chunks * bsz
    rb = _row_tile(rows, rows, cap=640)
    return pl.pallas_call(
        functools.partial(_s5_kernel, n_chunks=n_chunks, bsz=bsz, rb=rb),
        grid=(G_A,),
        in_specs=[pl.BlockSpec((None, rows, S5_W), lambda g: (g, 0, 0)),
                  pl.BlockSpec((None, S5_W, S5_W), lambda g: (g, 0, 0)),
                  pl.BlockSpec((None, S5_W, 4 * LANES), lambda g: (g, 0, 0)),
                  pl.BlockSpec((None, 4 * LANES, S5_W), lambda g: (g, 0, 0)),
                  pl.BlockSpec((None, SUBLANES, LANES), lambda g: (g, 0, 0))],
        out_specs=pl.BlockSpec((None, rows, S5_W), lambda g: (g, 0, 0)),
        out_shape=jax.ShapeDtypeStruct((G_A, rows, S5_W), BF16),
        scratch_shapes=[pltpu.VMEM((4, rows, LANES), F32), pltpu.VMEM((4, rows, LANES), F32)],
        compiler_params=_cparams(1),
        name="s5_chunks",
    )(ut, m, p, q, coef)


def _lru_weights(conv_w, conv_b, w_r, b_r, w_i, b_i, lam):
    ncb = D_B // LRU_CB
    hpb = LRU_CB // BW_B

    def blockdiag(w):
        w = w.astype(F32).reshape(2, ncb, hpb, BW_B, BW_B)
        out = jnp.zeros((2, ncb, hpb, BW_B, hpb, BW_B), F32)
        for k in range(hpb):
            out = out.at[:, :, k, :, k, :].set(w[:, :, k])
        return out.reshape(2, ncb, LRU_CB, LRU_CB)
    wg = jnp.concatenate([blockdiag(w_r), blockdiag(w_i)], axis=-1).astype(BF16)
    bg = jnp.concatenate([b_r.astype(F32).reshape(2, ncb, 1, LRU_CB),
                          b_i.astype(F32).reshape(2, ncb, 1, LRU_CB)], axis=-1)
    cl = (-LRU_C * LOG2E * jax.nn.softplus(-lam.astype(F32))).reshape(2, ncb, 1, LRU_CB)
    cw = jnp.pad(conv_w.astype(F32), ((0, SUBLANES - CONV_B), (0, 0)))
    cb = conv_b.astype(F32).reshape(1, D_B)
    return cw, cb, wg, bg, cl


LRU_HALO = BF16_ROWS
N_CB = D_B // LRU_CB


def _lru_kernel(fp_ref, fc_ref, fn_ref, bp_ref, bc_ref, bn_ref, cw_ref, cb_ref, wg_ref, bg_ref, cl_ref,
                hf_ref, hb_ref, xw, a_scr, b_scr, h_scr, init_scr, carry_scr, *, tt, n_tiles):
    t = pl.program_id(1)
    seg = tt // SUBLANES
    sb = tt // 4
    halo = LRU_HALO
    dirs = ((fp_ref, fc_ref, fn_ref, t), (bp_ref, bc_ref, bn_ref, n_tiles - 1 - t))

    @pl.when(t == 0)
    def _():
        carry_scr[...] = jnp.zeros_like(carry_scr)

    for d, (p_ref, c_ref, n_ref, ti) in enumerate(dirs):
        keep = jnp.where(ti < n_tiles - 1, 1.0, 0.0)
        xw[d, 0:halo, :] = p_ref[...].astype(F32)
        xw[d, halo:halo + tt, :] = c_ref[...].astype(F32)
        xw[d, halo + tt:2 * halo + tt, :] = n_ref[...].astype(F32) * keep

    def gates(k, _):
        n = sb + 2 * SUBLANES
        base = pl.multiple_of(k * sb, SUBLANES)
        rows = pl.ds(base, sb)
        for d in range(2):
            for cb in range(N_CB):
                lanes = slice(cb * LRU_CB, (cb + 1) * LRU_CB)
                win = xw[d, pl.ds(base + halo - SUBLANES, n), lanes]
                lo, hi = SUBLANES, SUBLANES + sb
                xc = (cw_ref[0:1, lanes] * pltpu.roll(win, 2, 0)[lo:hi]
                      + cw_ref[1:2, lanes] * pltpu.roll(win, 1, 0)[lo:hi]
                      + cw_ref[2:3, lanes] * win[lo:hi]
                      + cw_ref[3:4, lanes] * pltpu.roll(win, n - 1, 0)[lo:hi]) + cb_ref[0:1, lanes]
                z = jnp.dot(xc.astype(BF16), wg_ref[d, cb], preferred_element_type=F32) + bg_ref[d, cb]
                r = _sigmoid(z[:, :LRU_CB])
                ig = _sigmoid(z[:, LRU_CB:])
                a = jnp.exp2(cl_ref[d, cb] * r)
                y = 1.0 - a * a
                root = jnp.where(y > 0.0, y * lax.rsqrt(y), 0.0)
                a_scr[d, cb, rows, :] = a
                b_scr[d, cb, rows, :] = root * (ig * xc)
        return 0
    lax.fori_loop(0, tt // sb, gates, 0)

    for k in range(-(-PAD // tt)):
        npad = min(tt, PAD - k * tt)

        @pl.when(t == k)
        def _(npad=npad):
            for cb in range(N_CB):
                b_scr[0, cb, 0:npad, :] = jnp.zeros((npad, LANES), F32)

    chains = [(d, cb) for d in range(2) for cb in range(N_CB)]

    def rows_of(d, i):
        return pl.ds(i if d == 0 else seg - 1 - i, SUBLANES, stride=seg)

    def p1(i, c):
        out = []
        for n, (d, cb) in enumerate(chains):
            h, acc = c[2 * n], c[2 * n + 1]
            a = a_scr[d, cb, rows_of(d, i), :]
            out += [a * h + b_scr[d, cb, rows_of(d, i), :], a * acc]
        return tuple(out)

    z8 = jnp.zeros((SUBLANES, LANES), F32)
    o8 = jnp.ones((SUBLANES, LANES), F32)
    ends = lax.fori_loop(0, seg, p1, (z8, o8) * len(chains))
    for n, (d, cb) in enumerate(chains):
        he, ae = ends[2 * n], ends[2 * n + 1]
        c = carry_scr[d, cb, 0:1, :]
        for s in (range(SUBLANES) if d == 0 else reversed(range(SUBLANES))):
            init_scr[d, cb, s:s + 1, :] = c
            c = he[s:s + 1] + ae[s:s + 1] * c
        carry_scr[d, cb, 0:1, :] = c

    def p2(i, c):
        out = []
        for n, (d, cb) in enumerate(chains):
            h = a_scr[d, cb, rows_of(d, i), :] * c[n] + b_scr[d, cb, rows_of(d, i), :]
            h_scr[d, cb, rows_of(d, i), :] = h
            out.append(h)
        return tuple(out)
    lax.fori_loop(0, seg, p2, tuple(init_scr[d, cb] for d, cb in chains))

    for d, o_ref in enumerate((hf_ref, hb_ref)):
        for cb in range(N_CB):
            o_ref[:, cb * LRU_CB:(cb + 1) * LRU_CB] = h_scr[d, cb].astype(o_ref.dtype)


def _lru_mixer(x_b, weights, bsz, lp):
    cw, cb, wg, bg, cl = weights
    tt = _lru_tile(lp)
    n_tiles = lp // tt
    hpt = tt // LRU_HALO
    last = lp // LRU_HALO - 1
    x3 = x_b.reshape(bsz, lp, D_B)

    def specs(tile_of):
        return [pl.BlockSpec((None, LRU_HALO, D_B), lambda b, t: (b, jnp.maximum(tile_of(t) * hpt - 1, 0), 0)),
                pl.BlockSpec((None, tt, D_B), lambda b, t: (b, tile_of(t), 0)),
                pl.BlockSpec((None, LRU_HALO, D_B), lambda b, t: (b, jnp.minimum((tile_of(t) + 1) * hpt, last), 0))]
    fwd = specs(lambda t: t)
    bwd = specs(lambda t: n_tiles - 1 - t)

    def whole(a):
        return pl.BlockSpec(a.shape, lambda b, t: (0,) * a.ndim)
    hf, hb = pl.pallas_call(
        functools.partial(_lru_kernel, tt=tt, n_tiles=n_tiles),
        grid=(bsz, n_tiles),
        in_specs=fwd + bwd + [whole(cw), whole(cb), whole(wg), whole(bg), whole(cl)],
        out_specs=[fwd[1], bwd[1]],
        out_shape=[jax.ShapeDtypeStruct((bsz, lp, D_B), BF16)] * 2,
        scratch_shapes=[pltpu.VMEM((2, tt + 2 * LRU_HALO, D_B), F32),
                        pltpu.VMEM((2, N_CB, tt, LANES), F32), pltpu.VMEM((2, N_CB, tt, LANES), F32),
                        pltpu.VMEM((2, N_CB, tt, LANES), F32),
                        pltpu.VMEM((2, N_CB, SUBLANES, LANES), F32),
                        pltpu.VMEM((2, N_CB, SUBLANES, LANES), F32)],
        compiler_params=_cparams(2),
        name="rglru",
    )(x3, x3, x3, x3, x3, x3, cw, cb, wg, bg, cl)
    return hf.reshape(bsz * lp, D_B), hb.reshape(bsz * lp, D_B)


def _ab_out_kernel(*refs, lp, tm):
    n_cur = tm // BLOCK
    x_refs = refs[:n_cur]
    meta_ref, yt_ref, hf_ref, hb_ref, gb_ref, permt_ref, wglu_ref, bglu_ref, wo_ref, o_ref = refs[n_cur:]
    yb =((hf_ref[...].astype(F32) + hb_ref[...].astype(F32)) * _gelu(gb_ref[...].astype(F32))).astype(BF16)
    units = []
    for q in range(tm // UNIT):
        def load(g, half, q=q):
            return yt_ref[g, q * BF16_ROWS:(q + 1) * BF16_ROWS, half * LANES:(half + 1) * LANES].astype(F32)
        ysc = _chunks_to_tokens(load).astype(BF16)
        units.append(jnp.dot(permt_ref[...], ysc, preferred_element_type=F32))
    ya = _gelu(jnp.concatenate(units, axis=0))
    gate = _sigmoid(jnp.dot(ya.astype(BF16), wglu_ref[...], preferred_element_type=F32) + bglu_ref[...])
    ya = (ya * gate).astype(BF16)
    acc = jnp.dot(ya, wo_ref[0:D_A, :], preferred_element_type=F32)
    acc = acc + jnp.dot(yb, wo_ref[D_A:, :], preferred_element_type=F32)
    valid = _valid_rows(pl.program_id(0), tm, lp, acc.shape)
    o_ref[...] = jnp.where(valid, _padded_rows(x_refs, meta_ref, lp // BLOCK) + acc, 0.0)


def _ab_out(yt, hf, hb, g_b, x, meta_blk, w_glu, b_glu, w_out, lp):
    bsz, seq, d = x.shape
    rows = bsz * lp
    tm = _row_tile(rows, lp, mult=UNIT)
    x_specs, x_ops = _padded_row_specs(x, lp, tm)
    return pl.pallas_call(
        functools.partial(_ab_out_kernel, lp=lp, tm=tm),
        grid=(rows // tm,),
        in_specs=x_specs + [
                  pl.BlockSpec((G_A, tm // S5_T, S5_W), lambda i: (0, i, 0)),
                  pl.BlockSpec((tm, D_B), lambda i: (i, 0)),
                  pl.BlockSpec((tm, D_B), lambda i: (i, 0)),
                  pl.BlockSpec((tm, D_B), lambda i: (i, 0)),
                  pl.BlockSpec((UNIT, UNIT), lambda i: (0, 0)),
                  pl.BlockSpec((D_A, D_A), lambda i: (0, 0)),
                  pl.BlockSpec((1, D_A), lambda i: (0, 0)),
                  pl.BlockSpec((D_A + D_B, d), lambda i: (0, 0))],
        out_specs=pl.BlockSpec((tm, d), lambda i: (i, 0)),
        out_shape=jax.ShapeDtypeStruct((rows, d), F32),
        compiler_params=_cparams(1),
        name="ab_out",
    )(*x_ops, meta_blk, yt, hf, hb, g_b, _unit_perm().T, w_glu.astype(BF16),
      b_glu.astype(F32).reshape(1, D_A), w_out.astype(BF16))


def _proj_res_kernel(x_ref, h_ref, w_ref, o_ref, *, lp, tm):
    acc = jnp.dot(x_ref[...], w_ref[...], preferred_element_type=F32)
    valid = _valid_rows(pl.program_id(0), tm, lp, acc.shape)
    o_ref[...] = jnp.where(valid, h_ref[...] + acc, 0.0)


def _proj_res(x, h2, w, lp):
    rows, d = h2.shape
    tm = _row_tile(rows, lp)
    k = x.shape[1]
    return pl.pallas_call(
        functools.partial(_proj_res_kernel, lp=lp, tm=tm),
        grid=(rows // tm,),
        in_specs=[pl.BlockSpec((tm, k), lambda i: (i, 0)),
                  pl.BlockSpec((tm, d), lambda i: (i, 0)),
                  pl.BlockSpec((k, d), lambda i: (0, 0))],
        out_specs=pl.BlockSpec((tm, d), lambda i: (i, 0)),
        out_shape=jax.ShapeDtypeStruct((rows, d), F32),
        compiler_params=_cparams(1),
        name="proj_res",
    )(x, h2, w.astype(BF16))


LOG2E = math.log2(math.e)
N_PAIRS = N_Q_HEADS // 2


def _pair_heads():
    return [((2 * hp) * GQ + g, (2 * hp + 1) * GQ + g) for hp in range(N_KV_HEADS // 2) for g in range(GQ)]


def _attn_bias(lp):
    nb = lp // BLOCK
    assert nb >= 3
    qi = jnp.arange(BLOCK)[:, None]
    ki = jnp.arange(3 * BLOCK)[None, :]
    dist = jnp.abs(qi + BLOCK - ki)
    slopes = 2.0 ** (-8.0 * jnp.arange(1, N_Q_HEADS + 1, dtype=F32) / N_Q_HEADS)
    tables = []
    for n in (0, 1, 2 if nb > 3 else None, nb - 1):
        if n is None:
            tables.append(tables[-1])
            continue
        key_pos = (n - 1) * BLOCK + ki
        ok = (dist <= WINDOW) & (key_pos >= PAD) & (key_pos < lp)
        alibi = -slopes[:, None, None] * dist.astype(F32)[None] * LOG2E
        tables.append(jnp.where(ok[None], alibi, NEG))
    t = jnp.stack(tables)
    pairs = jnp.asarray(_pair_heads())
    return jnp.concatenate([t[:, pairs[:, 0]], t[:, pairs[:, 1]]], axis=-1)


def _attn_kernel(q_ref, kp_ref, kc_ref, kn_ref, vp_ref, vc_ref, vn_ref, bias_ref, sink_ref, o_ref,
                 k_scr, v_scr, s_scr, m_scr, p_scr):
    nk = 3 * BLOCK
    low_kv = lax.broadcasted_iota(jnp.int32, (nk, LANES), 1) < HEAD_DIM
    low_q = lax.broadcasted_iota(jnp.int32, (BLOCK, LANES), 1) < HEAD_DIM
    heads = _pair_heads()
    n_hp = N_KV_HEADS // 2
    zero = jnp.zeros((nk, LANES), BF16)
    ones_lo = jnp.where(low_kv, 1.0, 0.0).astype(BF16)
    ones_hi = jnp.where(low_kv, 0.0, 1.0).astype(BF16)
    for hp in range(n_hp):
        cols = slice(hp * LANES, (hp + 1) * LANES)
        kpair = jnp.concatenate([kp_ref[:, cols], kc_ref[:, cols], kn_ref[:, cols]], axis=0)
        vpair = jnp.concatenate([vp_ref[:, cols], vc_ref[:, cols], vn_ref[:, cols]], axis=0)
        k_scr[hp, 0:nk, :] = jnp.where(low_kv, kpair, zero)
        k_scr[hp, nk:, :] = jnp.where(low_kv, zero, kpair)
        v_scr[hp, 0:nk, 0:LANES] = jnp.where(low_kv, vpair, zero)
        v_scr[hp, nk:, 0:LANES] = jnp.where(low_kv, zero, vpair)
        v_scr[hp, 0:nk, LANES:] = ones_lo
        v_scr[hp, nk:, LANES:] = ones_hi
    pairs = [(hp, g) for hp in range(n_hp) for g in range(GQ)]
    for pair, (hp, g) in enumerate(pairs):
        qp = q_ref[:, pair * LANES:(pair + 1) * LANES]
        s = lax.dot_general(qp, k_scr[hp], (((1,), (1,)), ((), ())), preferred_element_type=F32)
        s_scr[pair] = s + bias_ref[pair]
    for pair in range(len(pairs)):
        for part in range(2):
            sink = sink_ref[0, heads[pair][part]]
            m = jnp.maximum(jnp.max(s_scr[pair, :, part * nk:(part + 1) * nk], axis=-1, keepdims=True), sink)
            m_scr[pair, part] = jnp.broadcast_to(m, (BLOCK, LANES))
    for pair in range(len(pairs)):
        for part in range(2):
            m = m_scr[pair, part]
            for c in range(3):
                cs = slice(part * nk + c * BLOCK, part * nk + (c + 1) * BLOCK)
                p_scr[pair, :, cs] = jnp.exp2(s_scr[pair, :, cs] - m).astype(BF16)
    for pair, (hp, g) in enumerate(pairs):
        oe = jnp.dot(p_scr[pair], v_scr[hp], preferred_element_type=F32)
        m = jnp.where(low_q, m_scr[pair, 0], m_scr[pair, 1])
        sink = jnp.where(low_q, sink_ref[0, heads[pair][0]], sink_ref[0, heads[pair][1]])
        o = oe[:, :LANES] / (oe[:, LANES:] + jnp.exp2(sink - m))
        o_ref[:, pair * LANES:(pair + 1) * LANES] = o.astype(o_ref.dtype)


def _attention(q, k, v, sink, bsz, lp):
    nb = lp // BLOCK
    dq = N_Q_HEADS * HEAD_DIM
    dkv = N_KV_HEADS * HEAD_DIM
    q3, k3, v3 = q.reshape(bsz, lp, dq), k.reshape(bsz, lp, dkv), v.reshape(bsz, lp, dkv)
    sink2 = (sink.astype(F32) * LOG2E).reshape(1, N_Q_HEADS)
    kv_prev = pl.BlockSpec((None, BLOCK, dkv), lambda b, n: (b, jnp.maximum(n - 1, 0), 0))
    kv_cur = pl.BlockSpec((None, BLOCK, dkv), lambda b, n: (b, n, 0))
    kv_next = pl.BlockSpec((None, BLOCK, dkv), lambda b, n: (b, jnp.minimum(n + 1, nb - 1), 0))

    def bias_case(b, n):
        return (jnp.where(n == 0, 0, jnp.where(n == 1, 1, jnp.where(n == nb - 1, 3, 2))), 0, 0, 0)
    out = pl.pallas_call(
        _attn_kernel,
        grid=(bsz, nb),
        in_specs=[pl.BlockSpec((None, BLOCK, dq), lambda b, n: (b, n, 0)),
                  kv_prev, kv_cur, kv_next, kv_prev, kv_cur, kv_next,
                  pl.BlockSpec((None, N_PAIRS, BLOCK, 6 * BLOCK), bias_case),
                  pl.BlockSpec(memory_space=pltpu.SMEM)],
        out_specs=pl.BlockSpec((None, BLOCK, dq), lambda b, n: (b, n, 0)),
        out_shape=jax.ShapeDtypeStruct((bsz, lp, dq), BF16),
        scratch_shapes=[pltpu.VMEM((N_KV_HEADS // 2, 6 * BLOCK, LANES), BF16),
                        pltpu.VMEM((N_KV_HEADS // 2, 6 * BLOCK, 2 * LANES), BF16),
                        pltpu.VMEM((N_PAIRS, BLOCK, 6 * BLOCK), F32),
                        pltpu.VMEM((N_PAIRS, 2, BLOCK, LANES), F32),
                        pltpu.VMEM((N_PAIRS, BLOCK, 6 * BLOCK), BF16)],
        compiler_params=_cparams(2),
        name="swa",
    )(q3, k3, k3, k3, v3, v3, v3, _attn_bias(lp), sink2)
    return out.reshape(bsz * lp, dq)


def _ffn_kernel(*refs, lp, tm, n_tiles, final, n_cur):
    hp_ref, hc_refs = refs[0], refs[1:1 + n_cur]
    hn_ref, g_ref, wup_ref, cw_ref, cb_ref, wdn_ref, gf_ref, o_ref, nat_scr, hx_scr, act_scr = refs[1 + n_cur:]
    i = pl.program_id(1 if final else 0)
    g = g_ref[...]
    hc = jnp.concatenate([r[...] for r in hc_refs], axis=0) if n_cur > 1 else hc_refs[0][...]
    n = tm + 2 * FFN_HALO
    nv = n // SUBLANES
    nslab = D_MODEL // LANES

    def put(r0, x):
        for l in range(nslab):
            nat_scr[l, r0:r0 + x.shape[0], :] = x[:, l * LANES:(l + 1) * LANES]
    keep = jnp.where(i < n_tiles - 1, 1.0, 0.0)
    put(0, _rmsnorm(hp_ref[...], g))
    put(FFN_HALO, _rmsnorm(hc, g))
    put(FFN_HALO + tm, _rmsnorm(hn_ref[...], g) * keep)

    def to_strided(jp, _):
        r = pl.ds(pl.multiple_of(jp * BF16_ROWS, BF16_ROWS), BF16_ROWS)
        for l in range(nslab):
            two = jnp.concatenate([nat_scr[l, pl.ds(2 * jp, SUBLANES, stride=nv), :],
                                   nat_scr[l, pl.ds(2 * jp + 1, SUBLANES, stride=nv), :]], axis=0)
            hx_scr[r, l * LANES:(l + 1) * LANES] = two.astype(BF16)
        return 0
    lax.fori_loop(0, nv // 2, to_strided, 0)

    def conv(u, cols):
        prev = jnp.concatenate([pltpu.roll(u[n - SUBLANES:], 1, 0), u[:n - SUBLANES]], axis=0)
        nxt = jnp.concatenate([u[SUBLANES:], pltpu.roll(u[:SUBLANES], SUBLANES - 1, 0)], axis=0)
        return (cw_ref[0:1, cols] * prev + cw_ref[1:2, cols] * u + cw_ref[2:3, cols] * nxt) + cb_ref[0:1, cols]

    for c in range(D_FF // FFN_TF):
        ca = slice(c * FFN_TF, (c + 1) * FFN_TF)
        cg = slice(D_FF + c * FFN_TF, D_FF + (c + 1) * FFN_TF)
        ua = jnp.dot(hx_scr[...], wup_ref[:, ca], preferred_element_type=F32)
        ug = jnp.dot(hx_scr[...], wup_ref[:, cg], preferred_element_type=F32)
        act_scr[:, ca] = (_gelu(conv(ug, cg)) * conv(ua, ca)).astype(BF16)

    ys = jnp.dot(act_scr[...], wdn_ref[...], preferred_element_type=F32)
    for j in range(nv):
        for l in range(nslab):
            nat_scr[l, pl.ds(j, SUBLANES, stride=nv), :] = ys[j * SUBLANES:(j + 1) * SUBLANES,
                                                              l * LANES:(l + 1) * LANES]
    y = hc + jnp.concatenate([nat_scr[l, FFN_HALO:FFN_HALO + tm, :] for l in range(nslab)], axis=1)
    if final:
        o_ref[...] = _rmsnorm(y, gf_ref[...])
    else:
        o_ref[...] = jnp.where(_valid_rows(i, tm, lp, y.shape), y, 0.0)


def _conv_ffn(h2, g, w_up, conv_w, conv_b, w_down, g_final, bsz, lp, final):
    rows, d = h2.shape
    const = dict(pipeline_mode=pl.Buffered(1))
    cw = jnp.pad(conv_w.astype(F32), ((0, SUBLANES - CONV_F), (0, 0)))
    if final:
        seq = lp - FRONT
        tm = _row_tile(seq, seq, cap=FFN_TM, mult=BLOCK)
        n_tiles, n_cur = seq // tm, tm // BLOCK
        grid = (bsz, n_tiles)
        last = lp // FFN_HALO - 1
        h_in = h2.reshape(bsz, lp, d)
        prev = pl.BlockSpec((None, FFN_HALO, d), lambda b, j: (b, (FRONT + j * tm) // FFN_HALO - 1, 0))
        cur = [pl.BlockSpec((None, BLOCK, d), lambda b, j, k=k: (b, 1 + j * n_cur + k, 0)) for k in range(n_cur)]
        nxt = pl.BlockSpec((None, FFN_HALO, d),
                           lambda b, j: (b, jnp.minimum((FRONT + (j + 1) * tm) // FFN_HALO, last), 0))
        out_spec = pl.BlockSpec((None, tm, d), lambda b, j: (b, j, 0))
        out_shape = jax.ShapeDtypeStruct((bsz, seq, d), F32)

        def whole(shape):
            return pl.BlockSpec(shape, lambda b, j: (0,) * len(shape))
    else:
        tm = _row_tile(rows, lp, cap=FFN_TM)
        n_tiles, n_cur = rows // tm, 1
        grid = (n_tiles,)
        hb = tm // FFN_HALO
        last = rows // FFN_HALO - 1
        h_in = h2
        prev = pl.BlockSpec((FFN_HALO, d), lambda i: (jnp.maximum(i * hb - 1, 0), 0))
        cur = [pl.BlockSpec((tm, d), lambda i: (i, 0))]
        nxt = pl.BlockSpec((FFN_HALO, d), lambda i: (jnp.minimum((i + 1) * hb, last), 0))
        out_spec = pl.BlockSpec((tm, d), lambda i: (i, 0))
        out_shape = jax.ShapeDtypeStruct((rows, d), F32)

        def whole(shape):
            return pl.BlockSpec(shape, lambda i: (0,) * len(shape))
    weights = [pl.BlockSpec((1, d), whole((1, d)).index_map),
               pl.BlockSpec((d, 2 * D_FF), whole((d, 2 * D_FF)).index_map, **const),
               whole((SUBLANES, 2 * D_FF)), whole((1, 2 * D_FF)),
               pl.BlockSpec((D_FF, d), whole((D_FF, d)).index_map, **const),
               whole((1, d))]
    return pl.pallas_call(
        functools.partial(_ffn_kernel, lp=lp, tm=tm, n_tiles=n_tiles, final=final, n_cur=n_cur),
        grid=grid,
        in_specs=[prev] + cur + [nxt] + weights,
        out_specs=out_spec,
        out_shape=out_shape,
        scratch_shapes=[pltpu.VMEM((d // LANES, tm + 2 * FFN_HALO, LANES), F32),
                        pltpu.VMEM((tm + 2 * FFN_HALO, d), BF16),
                        pltpu.VMEM((tm + 2 * FFN_HALO, D_FF), BF16)],
        compiler_params=_cparams(len(grid)),
        name="conv_ffn",
    )(*([h_in] * (2 + n_cur)), g.astype(F32).reshape(1, d), w_up.astype(BF16), cw,
      conv_b.astype(F32).reshape(1, 2 * D_FF), w_down.astype(BF16), g_final.astype(F32).reshape(1, d))


def _trunk(x, prm):
    bsz, seq, d = x.shape
    lp = seq + FRONT
    assert seq % BLOCK == 0 and d == D_MODEL

    x = x.astype(F32)
    meta_blk = jnp.pad(prm['meta_tokens'].astype(F32), ((PAD, 0), (0, 0)))
    u_a, x_b, g_b = _in_proj(x, meta_blk, prm['norm_mix_g'][0], prm['w_in_ab'][0].astype(BF16), lp)
    s5w = _s5_weights(prm['s5_lambda_re'][0], prm['s5_lambda_im'][0], prm['s5_log_dt'][0],
                      prm['s5_b_re'][0], prm['s5_b_im'][0], prm['s5_c_re'][0], prm['s5_c_im'][0], prm['s5_d'][0])
    y_a = _s5_mixer(u_a, s5w, bsz, lp)
    lruw = _lru_weights(prm['lru_conv_w'][0], prm['lru_conv_b'][0], prm['lru_w_r'][0], prm['lru_b_r'][0],
                        prm['lru_w_i'][0], prm['lru_b_i'][0], prm['lru_lambda'][0])
    hf, hb = _lru_mixer(x_b, lruw, bsz, lp)
    h = _ab_out(y_a, hf, hb, g_b, x, meta_blk, prm['w_glu'][0], prm['b_glu'][0], prm['w_out_ab'][0], lp)
    h = _conv_ffn(h, prm['norm_ffn_g'][0], prm['w_up'][0], prm['ffn_conv_w'][0], prm['ffn_conv_b'][0],
                  prm['w_down'][0], prm['final_norm_g'], bsz, lp, final=False)

    dq, dkv = N_Q_HEADS * HEAD_DIM, N_KV_HEADS * HEAD_DIM
    head_order = jnp.asarray([h_ for pair in _pair_heads() for h_ in pair])
    col_order = (head_order[:, None] * HEAD_DIM + jnp.arange(HEAD_DIM)[None, :]).reshape(-1)
    w_qkv = prm['w_qkv'][0]
    w_qkv = jnp.concatenate([w_qkv[:, :dq][:, col_order], w_qkv[:, dq:]], axis=1).astype(BF16)
    q, k, v = _norm_mm(h, prm['norm_mix_g'][1], w_qkv, (dq, dkv, dkv),
                       (LOG2E / math.sqrt(HEAD_DIM), 1.0, 1.0), lp)
    o = _attention(q, k, v, prm['attn_sink'][0], bsz, lp)
    h = _proj_res(o, h, prm['w_o'][0][col_order], lp)
    return _conv_ffn(h, prm['norm_ffn_g'][1], prm['w_up'][1], prm['ffn_conv_w'][1], prm['ffn_conv_b'][1],
                     prm['w_down'][1], prm['final_norm_g'], bsz, lp, final=True)


def kernel(x_prompt, x_sample, meta_tokens, norm_mix_g, norm_ffn_g, final_norm_g, w_in_ab, s5_lambda_re, s5_lambda_im, s5_log_dt, s5_b_re, s5_b_im, s5_c_re, s5_c_im, s5_d, w_glu, b_glu, lru_conv_w, lru_conv_b, lru_w_r, lru_b_r, lru_w_i, lru_b_i, lru_lambda, w_out_ab, w_qkv, w_o, attn_sink, w_up, ffn_conv_w, ffn_conv_b, w_down):
    prm = dict(meta_tokens=meta_tokens, norm_mix_g=norm_mix_g, norm_ffn_g=norm_ffn_g, final_norm_g=final_norm_g,
               w_in_ab=w_in_ab, s5_lambda_re=s5_lambda_re, s5_lambda_im=s5_lambda_im, s5_log_dt=s5_log_dt,
               s5_b_re=s5_b_re, s5_b_im=s5_b_im, s5_c_re=s5_c_re, s5_c_im=s5_c_im, s5_d=s5_d, w_glu=w_glu,
               b_glu=b_glu, lru_conv_w=lru_conv_w, lru_conv_b=lru_conv_b, lru_w_r=lru_w_r, lru_b_r=lru_b_r,
               lru_w_i=lru_w_i, lru_b_i=lru_b_i, lru_lambda=lru_lambda, w_out_ab=w_out_ab, w_qkv=w_qkv,
               w_o=w_o, attn_sink=attn_sink, w_up=w_up, ffn_conv_w=ffn_conv_w, ffn_conv_b=ffn_conv_b,
               w_down=w_down)
    return (_trunk(x_prompt, prm), _trunk(x_sample, prm))
```

```python
import functools
import math

import jax
import jax.numpy as jnp
from jax import lax
from jax.experimental import pallas as pl
from jax.experimental.pallas import tpu as pltpu

D_MODEL = 1024
N_META = 16
D_A = 512
S5_GROUP = 16
G_A = D_A // S5_GROUP
S5_STATE = 64
D_B = 512
H_B = 8
BW_B = D_B // H_B
LRU_C = 8.0
CONV_B = 4
CONV_B_LEFT = 2
HEAD_DIM = 64
N_Q_HEADS = 16
N_KV_HEADS = 4
GQ = N_Q_HEADS // N_KV_HEADS
WINDOW = 128
BLOCK = 128
D_FF = 2816
CONV_F = 3
EPS = 1e-6
NEG = -1e30

PAD = BLOCK - N_META
FRONT = PAD + N_META
S5_T = 16
S5_W = S5_T * S5_GROUP
LANES = 128
SUBLANES = 8
BF16_ROWS = 16
FFN_HALO = BF16_ROWS
FFN_TF = 256
FFN_TM = 1024
LRU_CB = LANES
VMEM_LIMIT = 56 * 1024 * 1024

F32 = jnp.float32
BF16 = jnp.bfloat16


def _cparams(n_axes):
    return pltpu.CompilerParams(
        dimension_semantics=("arbitrary",) * n_axes, vmem_limit_bytes=VMEM_LIMIT)


def _row_tile(rows, lp, cap=512, mult=BF16_ROWS):
    best = None
    t = mult
    while t <= min(cap, lp):
        if rows % t == 0:
            best = t
        t += mult
    assert best is not None, (rows, lp, cap, mult)
    return best


def _lru_tile(lp):
    q = lp // 32
    assert lp % 32 == 0
    odd = q
    while odd % 2 == 0:
        odd //= 2
    best = 1
    for d in range(1, odd + 1, 2):
        if odd % d == 0 and 32 * d <= 640:
            best = d
    return 32 * best


def _rmsnorm(x, g):
    ms = jnp.mean(x * x, axis=-1, keepdims=True)
    return x * lax.rsqrt(ms + EPS) * g


def _gelu(x):
    k1 = -2.0 * math.sqrt(2.0 / math.pi) * math.log2(math.e)
    return x / (1.0 + jnp.exp2(x * (k1 + (k1 * 0.044715) * (x * x))))


def _sigmoid(x):
    return 1.0 / (1.0 + jnp.exp(-x))


def _valid_rows(tile_idx, tm, lp, shape):
    pos = lax.rem(tile_idx * tm, lp) + lax.broadcasted_iota(jnp.int32, shape, 0)
    pos = jnp.where(pos >= lp, pos - lp, pos)
    return pos >= PAD


def _norm_mm_kernel(h_ref, g_ref, w_ref, *o_refs, scales):
    hn = _rmsnorm(h_ref[...], g_ref[...]).astype(BF16)
    off = 0
    for o_ref, sc in zip(o_refs, scales):
        n = o_ref.shape[-1]
        z = jnp.dot(hn, w_ref[:, off:off + n], preferred_element_type=F32)
        if sc != 1.0:
            z = z * sc
        o_ref[...] = z.astype(o_ref.dtype)
        off += n


def _norm_mm(h2, g, w, splits, scales, lp):
    rows, d = h2.shape
    tm = _row_tile(rows, lp)
    n = w.shape[1]
    return pl.pallas_call(
        functools.partial(_norm_mm_kernel, scales=scales),
        grid=(rows // tm,),
        in_specs=[pl.BlockSpec((tm, d), lambda i: (i, 0)),
                  pl.BlockSpec((1, d), lambda i: (0, 0)),
                  pl.BlockSpec((d, n), lambda i: (0, 0))],
        out_specs=[pl.BlockSpec((tm, s), lambda i: (i, 0)) for s in splits],
        out_shape=[jax.ShapeDtypeStruct((rows, s), BF16) for s in splits],
        compiler_params=_cparams(1),
        name="norm_mm",
    )(h2, g.reshape(1, d), w)


UNIT = S5_T * BF16_ROWS
GRP_PER_VREG = LANES // S5_GROUP


def _unit_perm():
    r = jnp.arange(UNIT)
    src = (r % BF16_ROWS) * S5_T + r // BF16_ROWS
    return (src[:, None] == jnp.arange(UNIT)[None, :]).astype(BF16)


def _tokens_to_chunks(zp, store):
    grp = lax.broadcasted_iota(jnp.int32, (BF16_ROWS, LANES), 1) // S5_GROUP
    for j in range(D_A // LANES):
        v = [zp[s * BF16_ROWS:(s + 1) * BF16_ROWS, j * LANES:(j + 1) * LANES] for s in range(S5_T)]
        for gl in range(GRP_PER_VREG):
            for half in range(S5_T // GRP_PER_VREG):
                acc = None
                for p in range(GRP_PER_VREG):
                    src = v[half * GRP_PER_VREG + p]
                    shift = S5_GROUP * ((p - gl) % GRP_PER_VREG)
                    val = src if shift == 0 else pltpu.roll(src, shift, 1)
                    acc = val if acc is None else jnp.where(grp == p, val, acc)
                store(j * GRP_PER_VREG + gl, half, acc)


def _chunks_to_tokens(load):
    grp = lax.broadcasted_iota(jnp.int32, (BF16_ROWS, LANES), 1) // S5_GROUP
    cols = []
    for j in range(D_A // LANES):
        rows = []
        for half in range(S5_T // GRP_PER_VREG):
            y = [load(j * GRP_PER_VREG + gl, half) for gl in range(GRP_PER_VREG)]
            for p in range(GRP_PER_VREG):
                acc = None
                for gl in range(GRP_PER_VREG):
                    shift = S5_GROUP * ((gl - p) % GRP_PER_VREG)
                    val = y[gl] if shift == 0 else pltpu.roll(y[gl], shift, 1)
                    acc = val if acc is None else jnp.where(grp == gl, val, acc)
                rows.append(acc)
        cols.append(jnp.concatenate(rows, axis=0))
    return jnp.concatenate(cols, axis=1)


def _in_proj_kernel(*refs, tm, nb):
    n_cur = tm // BLOCK
    x_refs = refs[:n_cur]
    meta_ref, g_ref, w_ref, perm_ref, h_ref, u_ref, xb_ref, gb_ref = refs[n_cur:]
    first = pl.program_id(0) * n_cur
    blocks = [jnp.where(lax.rem(first + k, nb) == 0, meta_ref[...], x_refs[k][...]) for k in range(n_cur)]
    h = jnp.concatenate(blocks, axis=0)
    h_ref[...] = h
    hn = _rmsnorm(h, g_ref[...]).astype(BF16)
    xb_ref[...] = jnp.dot(hn, w_ref[:, D_A:D_A + D_B], preferred_element_type=F32).astype(BF16)
    gb_ref[...] = jnp.dot(hn, w_ref[:, D_A + D_B:], preferred_element_type=F32).astype(BF16)
    za = jnp.dot(hn, w_ref[:, 0:D_A], preferred_element_type=F32).astype(BF16)
    for q in range(tm // UNIT):
        zp = jnp.dot(perm_ref[...], za[q * UNIT:(q + 1) * UNIT], preferred_element_type=F32)

        def store(g, half, x, q=q):
            u_ref[g, q * BF16_ROWS:(q + 1) * BF16_ROWS, half * LANES:(half + 1) * LANES] = x.astype(BF16)
        _tokens_to_chunks(zp, store)


def _in_proj(x, meta, g, w, lp):
    bsz, seq, d = x.shape
    rows = bsz * lp
    nb = lp // BLOCK
    tm = _row_tile(rows, lp, mult=UNIT)
    n_cur = tm // BLOCK
    n = w.shape[1]
    meta_blk = jnp.pad(meta.astype(F32), ((PAD, 0), (0, 0)))
    x_specs = [pl.BlockSpec((None, BLOCK, d),
                            lambda i, k=k: ((i * n_cur + k) // nb, jnp.maximum((i * n_cur + k) % nb - 1, 0), 0))
               for k in range(n_cur)]
    return pl.pallas_call(
        functools.partial(_in_proj_kernel, tm=tm, nb=nb),
        grid=(rows // tm,),
        in_specs=x_specs + [pl.BlockSpec((BLOCK, d), lambda i: (0, 0)),
                            pl.BlockSpec((1, d), lambda i: (0, 0)),
                            pl.BlockSpec((d, n), lambda i: (0, 0)),
                            pl.BlockSpec((UNIT, UNIT), lambda i: (0, 0))],
        out_specs=[pl.BlockSpec((tm, d), lambda i: (i, 0)),
                   pl.BlockSpec((G_A, tm // S5_T, S5_W), lambda i: (0, i, 0)),
                   pl.BlockSpec((tm, D_B), lambda i: (i, 0)),
                   pl.BlockSpec((tm, D_B), lambda i: (i, 0))],
        out_shape=[jax.ShapeDtypeStruct((rows, d), F32),
                   jax.ShapeDtypeStruct((G_A, rows // S5_T, S5_W), BF16),
                   jax.ShapeDtypeStruct((rows, D_B), BF16),
                   jax.ShapeDtypeStruct((rows, D_B), BF16)],
        compiler_params=_cparams(1),
        name="in_proj",
    )(*([x.astype(F32)] * n_cur), meta_blk, g.reshape(1, d), w, _unit_perm())


def _cmul(ar, ai, br, bi):
    return ar * br - ai * bi, ar * bi + ai * br


def _s5_weights(lam_re, lam_im, log_dt, b_re, b_im, c_re, c_im, d_skip):
    hp = lax.Precision.HIGHEST
    t = S5_T
    lr = jnp.minimum(lam_re.astype(F32), -1e-4)
    li = lam_im.astype(F32)
    dt = jnp.exp(log_dt.astype(F32))[..., None]
    mag = jnp.exp(lr * dt)
    lbr, lbi = mag * jnp.cos(li * dt), mag * jnp.sin(li * dt)
    den = lr * lr + li * li
    xr, xi = lbr - 1.0, lbi
    fr, fi = (xr * lr + xi * li) / den, (xi * lr - xr * li) / den
    bbr, bbi = _cmul(fr[..., None], fi[..., None], b_re.astype(F32), b_im.astype(F32))
    pr, pi = [jnp.ones_like(lbr)], [jnp.zeros_like(lbi)]
    for _ in range(t):
        nr, ni = _cmul(pr[-1], pi[-1], lbr, lbi)
        pr.append(nr)
        pi.append(ni)
    pwr, pwi = jnp.stack(pr, -1), jnp.stack(pi, -1)
    cr, ci = c_re.astype(F32), c_im.astype(F32)
    wr, wi = _cmul(cr[..., None], ci[..., None], pwr[:, :, None], pwi[:, :, None])
    kk = (jnp.einsum('dgonk,dgni->dgkio', wr, bbr, precision=hp)
          - jnp.einsum('dgonk,dgni->dgkio', wi, bbi, precision=hp))[:, :, :t]
    eye = jnp.eye(S5_GROUP, dtype=F32) * d_skip.astype(F32).reshape(G_A, 1, S5_GROUP)
    k_all = jnp.concatenate([kk[1][:, 1:][:, ::-1],
                             (kk[0][:, 0] + kk[1][:, 0] + eye)[:, None],
                             kk[0][:, 1:]], axis=1)
    sidx = jnp.arange(t)
    lag = sidx[None, :] - sidx[:, None] + (t - 1)
    m = k_all[:, lag]
    m = m.transpose(0, 1, 3, 2, 4).reshape(G_A, S5_W, S5_W)
    pfr, pfi = _cmul(pwr[0][..., :t][..., ::-1][..., None], pwi[0][..., :t][..., ::-1][..., None],
                     bbr[0][:, :, None, :], bbi[0][:, :, None, :])
    pbr, pbi = _cmul(pwr[1][..., :t][..., None], pwi[1][..., :t][..., None],
                     bbr[1][:, :, None, :], bbi[1][:, :, None, :])

    def p_cols(x):
        x = x.transpose(0, 2, 3, 1).reshape(G_A, S5_W, S5_STATE)
        return jnp.pad(x, ((0, 0), (0, 0), (0, LANES - S5_STATE)))
    p = jnp.concatenate([p_cols(pfr), p_cols(pfi), p_cols(pbr), p_cols(pbi)], axis=-1)

    def q_rows(x):
        x = x.transpose(0, 2, 3, 1).reshape(G_A, S5_STATE, S5_W)
        return jnp.pad(x, ((0, 0), (0, LANES - S5_STATE), (0, 0)))
    q = jnp.concatenate([q_rows(wr[0][..., 1:]), q_rows(-wi[0][..., 1:]),
                         q_rows(wr[1][..., 1:][..., ::-1]), q_rows(-wi[1][..., 1:][..., ::-1])], axis=1)

    def c_row(x):
        return jnp.pad(x, ((0, 0), (0, LANES - S5_STATE)))[:, None]
    coef = jnp.concatenate([c_row(pwr[0][..., t]), c_row(pwi[0][..., t]),
                            c_row(pwr[1][..., t]), c_row(pwi[1][..., t]),
                            jnp.zeros((G_A, 4, LANES), F32)], axis=1)
    return m.astype(BF16), p.astype(BF16), q.astype(BF16), coef


def _s5_kernel(u_ref, m_ref, p_ref, q_ref, coef_ref, y_ref, s_scr, x0_scr, *, n_chunks, bsz, rb):
    rows = n_chunks * bsz
    ncomp = 4
    for k in range(rows // rb):
        sl = slice(k * rb, (k + 1) * rb)
        s = jnp.dot(u_ref[sl, :], p_ref[...], preferred_element_type=F32)
        for c in range(ncomp):
            s_scr[c, sl, :] = s[:, c * LANES:(c + 1) * LANES]

    shp = (bsz, LANES)
    lfr = jnp.broadcast_to(coef_ref[0:1, :], shp)
    lfi = jnp.broadcast_to(coef_ref[1:2, :], shp)
    lbr = jnp.broadcast_to(coef_ref[2:3, :], shp)
    lbi = jnp.broadcast_to(coef_ref[3:4, :], shp)

    def step(j, carry):
        xfr, xfi, xbr, xbi = carry
        rf = pl.ds(j, bsz, stride=n_chunks)
        rbk = pl.ds(n_chunks - 1 - j, bsz, stride=n_chunks)
        x0_scr[0, rf, :] = xfr
        x0_scr[1, rf, :] = xfi
        x0_scr[2, rbk, :] = xbr
        x0_scr[3, rbk, :] = xbi
        nfr = lfr * xfr - lfi * xfi + s_scr[0, rf, :]
        nfi = lfr * xfi + lfi * xfr + s_scr[1, rf, :]
        nbr = lbr * xbr - lbi * xbi + s_scr[2, rbk, :]
        nbi = lbr * xbi + lbi * xbr + s_scr[3, rbk, :]
        return nfr, nfi, nbr, nbi

    z = jnp.zeros(shp, F32)
    lax.fori_loop(0, n_chunks, step, (z, z, z, z), unroll=4 if n_chunks % 4 == 0 else 1)

    for k in range(rows // rb):
        sl = slice(k * rb, (k + 1) * rb)
        x0 = jnp.concatenate([x0_scr[c, sl, :] for c in range(ncomp)], axis=1).astype(BF16)
        y = jnp.dot(u_ref[sl, :], m_ref[...], preferred_element_type=F32)
        y = y + jnp.dot(x0, q_ref[...], preferred_element_type=F32)
        y_ref[sl, :] = y.astype(y_ref.dtype)


def _s5_mixer(ut, weights, bsz, lp):
    m, p, q, coef = weights
    n_chunks = lp // S5_T
    rows = n_chunks * bsz
    rb = _row_tile(rows, rows, cap=640)
    return pl.pallas_call(
        functools.partial(_s5_kernel, n_chunks=n_chunks, bsz=bsz, rb=rb),
        grid=(G_A,),
        in_specs=[pl.BlockSpec((None, rows, S5_W), lambda g: (g, 0, 0)),
                  pl.BlockSpec((None, S5_W, S5_W), lambda g: (g, 0, 0)),
                  pl.BlockSpec((None, S5_W, 4 * LANES), lambda g: (g, 0, 0)),
                  pl.BlockSpec((None, 4 * LANES, S5_W), lambda g: (g, 0, 0)),
                  pl.BlockSpec((None, SUBLANES, LANES), lambda g: (g, 0, 0))],
        out_specs=pl.BlockSpec((None, rows, S5_W), lambda g: (g, 0, 0)),
        out_shape=jax.ShapeDtypeStruct((G_A, rows, S5_W), BF16),
        scratch_shapes=[pltpu.VMEM((4, rows, LANES), F32), pltpu.VMEM((4, rows, LANES), F32)],
        compiler_params=_cparams(1),
        name="s5_chunks",
    )(ut, m, p, q, coef)


def _lru_weights(conv_w, conv_b, w_r, b_r, w_i, b_i, lam):
    ncb = D_B // LRU_CB
    hpb = LRU_CB // BW_B

    def blockdiag(w):
        w = w.astype(F32).reshape(2, ncb, hpb, BW_B, BW_B)
        out = jnp.zeros((2, ncb, hpb, BW_B, hpb, BW_B), F32)
        for k in range(hpb):
            out = out.at[:, :, k, :, k, :].set(w[:, :, k])
        return out.reshape(2, ncb, LRU_CB, LRU_CB)
    wg = jnp.concatenate([blockdiag(w_r), blockdiag(w_i)], axis=-1).astype(BF16)
    bg = jnp.concatenate([b_r.astype(F32).reshape(2, ncb, 1, LRU_CB),
                          b_i.astype(F32).reshape(2, ncb, 1, LRU_CB)], axis=-1)
    cl = (-LRU_C * LOG2E * jax.nn.softplus(-lam.astype(F32))).reshape(2, ncb, 1, LRU_CB)
    cw = jnp.pad(conv_w.astype(F32), ((0, SUBLANES - CONV_B), (0, 0)))
    cb = conv_b.astype(F32).reshape(1, D_B)
    return cw, cb, wg, bg, cl


LRU_HALO = BF16_ROWS
N_CB = D_B // LRU_CB


def _lru_kernel(fp_ref, fc_ref, fn_ref, bp_ref, bc_ref, bn_ref, cw_ref, cb_ref, wg_ref, bg_ref, cl_ref,
                hf_ref, hb_ref, xw, a_scr, b_scr, h_scr, init_scr, carry_scr, *, tt, n_tiles):
    t = pl.program_id(1)
    seg = tt // SUBLANES
    sb = tt // 4
    halo = LRU_HALO
    dirs = ((fp_ref, fc_ref, fn_ref, t), (bp_ref, bc_ref, bn_ref, n_tiles - 1 - t))

    @pl.when(t == 0)
    def _():
        carry_scr[...] = jnp.zeros_like(carry_scr)

    for d, (p_ref, c_ref, n_ref, ti) in enumerate(dirs):
        keep = jnp.where(ti < n_tiles - 1, 1.0, 0.0)
        xw[d, 0:halo, :] = p_ref[...].astype(F32)
        xw[d, halo:halo + tt, :] = c_ref[...].astype(F32)
        xw[d, halo + tt:2 * halo + tt, :] = n_ref[...].astype(F32) * keep

    def gates(k, _):
        n = sb + 2 * SUBLANES
        base = pl.multiple_of(k * sb, SUBLANES)
        rows = pl.ds(base, sb)
        for d in range(2):
            for cb in range(N_CB):
                lanes = slice(cb * LRU_CB, (cb + 1) * LRU_CB)
                win = xw[d, pl.ds(base + halo - SUBLANES, n), lanes]
                lo, hi = SUBLANES, SUBLANES + sb
                xc = (cw_ref[0:1, lanes] * pltpu.roll(win, 2, 0)[lo:hi]
                      + cw_ref[1:2, lanes] * pltpu.roll(win, 1, 0)[lo:hi]
                      + cw_ref[2:3, lanes] * win[lo:hi]
                      + cw_ref[3:4, lanes] * pltpu.roll(win, n - 1, 0)[lo:hi]) + cb_ref[0:1, lanes]
                z = jnp.dot(xc.astype(BF16), wg_ref[d, cb], preferred_element_type=F32) + bg_ref[d, cb]
                r = _sigmoid(z[:, :LRU_CB])
                ig = _sigmoid(z[:, LRU_CB:])
                a = jnp.exp2(cl_ref[d, cb] * r)
                y = 1.0 - a * a
                root = jnp.where(y > 0.0, y * lax.rsqrt(y), 0.0)
                a_scr[d, cb, rows, :] = a
                b_scr[d, cb, rows, :] = root * (ig * xc)
        return 0
    lax.fori_loop(0, tt // sb, gates, 0)

    for k in range(-(-PAD // tt)):
        npad = min(tt, PAD - k * tt)

        @pl.when(t == k)
        def _(npad=npad):
            for cb in range(N_CB):
                b_scr[0, cb, 0:npad, :] = jnp.zeros((npad, LANES), F32)

    chains = [(d, cb) for d in range(2) for cb in range(N_CB)]

    def rows_of(d, i):
        return pl.ds(i if d == 0 else seg - 1 - i, SUBLANES, stride=seg)

    def p1(i, c):
        out = []
        for n, (d, cb) in enumerate(chains):
            h, acc = c[2 * n], c[2 * n + 1]
            a = a_scr[d, cb, rows_of(d, i), :]
            out += [a * h + b_scr[d, cb, rows_of(d, i), :], a * acc]
        return tuple(out)

    z8 = jnp.zeros((SUBLANES, LANES), F32)
    o8 = jnp.ones((SUBLANES, LANES), F32)
    ends = lax.fori_loop(0, seg, p1, (z8, o8) * len(chains))
    for n, (d, cb) in enumerate(chains):
        he, ae = ends[2 * n], ends[2 * n + 1]
        c = carry_scr[d, cb, 0:1, :]
        for s in (range(SUBLANES) if d == 0 else reversed(range(SUBLANES))):
            init_scr[d, cb, s:s + 1, :] = c
            c = he[s:s + 1] + ae[s:s + 1] * c
        carry_scr[d, cb, 0:1, :] = c

    def p2(i, c):
        out = []
        for n, (d, cb) in enumerate(chains):
            h = a_scr[d, cb, rows_of(d, i), :] * c[n] + b_scr[d, cb, rows_of(d, i), :]
            h_scr[d, cb, rows_of(d, i), :] = h
            out.append(h)
        return tuple(out)
    lax.fori_loop(0, seg, p2, tuple(init_scr[d, cb] for d, cb in chains))

    for d, o_ref in enumerate((hf_ref, hb_ref)):
        for cb in range(N_CB):
            o_ref[:, cb * LRU_CB:(cb + 1) * LRU_CB] = h_scr[d, cb].astype(o_ref.dtype)


def _lru_mixer(x_b, weights, bsz, lp):
    cw, cb, wg, bg, cl = weights
    tt = _lru_tile(lp)
    n_tiles = lp // tt
    hpt = tt // LRU_HALO
    last = lp // LRU_HALO - 1
    x3 = x_b.reshape(bsz, lp, D_B)

    def specs(tile_of):
        return [pl.BlockSpec((None, LRU_HALO, D_B), lambda b, t: (b, jnp.maximum(tile_of(t) * hpt - 1, 0), 0)),
                pl.BlockSpec((None, tt, D_B), lambda b, t: (b, tile_of(t), 0)),
                pl.BlockSpec((None, LRU_HALO, D_B), lambda b, t: (b, jnp.minimum((tile_of(t) + 1) * hpt, last), 0))]
    fwd = specs(lambda t: t)
    bwd = specs(lambda t: n_tiles - 1 - t)

    def whole(a):
        return pl.BlockSpec(a.shape, lambda b, t: (0,) * a.ndim)
    hf, hb = pl.pallas_call(
        functools.partial(_lru_kernel, tt=tt, n_tiles=n_tiles),
        grid=(bsz, n_tiles),
        in_specs=fwd + bwd + [whole(cw), whole(cb), whole(wg), whole(bg), whole(cl)],
        out_specs=[fwd[1], bwd[1]],
        out_shape=[jax.ShapeDtypeStruct((bsz, lp, D_B), BF16)] * 2,
        scratch_shapes=[pltpu.VMEM((2, tt + 2 * LRU_HALO, D_B), F32),
                        pltpu.VMEM((2, N_CB, tt, LANES), F32), pltpu.VMEM((2, N_CB, tt, LANES), F32),
                        pltpu.VMEM((2, N_CB, tt, LANES), F32),
                        pltpu.VMEM((2, N_CB, SUBLANES, LANES), F32),
                        pltpu.VMEM((2, N_CB, SUBLANES, LANES), F32)],
        compiler_params=_cparams(2),
        name="rglru",
    )(x3, x3, x3, x3, x3, x3, cw, cb, wg, bg, cl)
    return hf.reshape(bsz * lp, D_B), hb.reshape(bsz * lp, D_B)


def _ab_out_kernel(yt_ref, hf_ref, hb_ref, gb_ref, h_ref, permt_ref, wglu_ref, bglu_ref, wo_ref, o_ref, *, lp, tm):
    yb = ((hf_ref[...].astype(F32) + hb_ref[...].astype(F32)) * _gelu(gb_ref[...].astype(F32))).astype(BF16)
    units = []
    for q in range(tm // UNIT):
        def load(g, half, q=q):
            return yt_ref[g, q * BF16_ROWS:(q + 1) * BF16_ROWS, half * LANES:(half + 1) * LANES].astype(F32)
        ysc = _chunks_to_tokens(load).astype(BF16)
        units.append(jnp.dot(permt_ref[...], ysc, preferred_element_type=F32))
    ya = _gelu(jnp.concatenate(units, axis=0))
    gate = _sigmoid(jnp.dot(ya.astype(BF16), wglu_ref[...], preferred_element_type=F32) + bglu_ref[...])
    ya = (ya * gate).astype(BF16)
    acc = jnp.dot(ya, wo_ref[0:D_A, :], preferred_element_type=F32)
    acc = acc + jnp.dot(yb, wo_ref[D_A:, :], preferred_element_type=F32)
    valid = _valid_rows(pl.program_id(0), tm, lp, acc.shape)
    o_ref[...] = jnp.where(valid, h_ref[...] + acc, 0.0)


def _ab_out(yt, hf, hb, g_b, h2, w_glu, b_glu, w_out, lp):
    rows, d = h2.shape
    tm = _row_tile(rows, lp, mult=UNIT)
    return pl.pallas_call(
        functools.partial(_ab_out_kernel, lp=lp, tm=tm),
        grid=(rows // tm,),
        in_specs=[pl.BlockSpec((G_A, tm // S5_T, S5_W), lambda i: (0, i, 0)),
                  pl.BlockSpec((tm, D_B), lambda i: (i, 0)),
                  pl.BlockSpec((tm, D_B), lambda i: (i, 0)),
                  pl.BlockSpec((tm, D_B), lambda i: (i, 0)),
                  pl.BlockSpec((tm, d), lambda i: (i, 0)),
                  pl.BlockSpec((UNIT, UNIT), lambda i: (0, 0)),
                  pl.BlockSpec((D_A, D_A), lambda i: (0, 0)),
                  pl.BlockSpec((1, D_A), lambda i: (0, 0)),
                  pl.BlockSpec((D_A + D_B, d), lambda i: (0, 0))],
        out_specs=pl.BlockSpec((tm, d), lambda i: (i, 0)),
        out_shape=jax.ShapeDtypeStruct((rows, d), F32),
        compiler_params=_cparams(1),
        name="ab_out",
    )(yt, hf, hb, g_b, h2, _unit_perm().T, w_glu.astype(BF16), b_glu.astype(F32).reshape(1, D_A),
      w_out.astype(BF16))


LOG2E = math.log2(math.e)
N_PAIRS = N_Q_HEADS // 2


def _pair_heads():
    return [((2 * hp) * GQ + g, (2 * hp + 1) * GQ + g) for hp in range(N_KV_HEADS // 2) for g in range(GQ)]


def _attn_bias(lp):
    nb = lp // BLOCK
    assert nb >= 3
    qi = jnp.arange(BLOCK)[:, None]
    ki = jnp.arange(3 * BLOCK)[None, :]
    dist = jnp.abs(qi + BLOCK - ki)
    slopes = 2.0 ** (-8.0 * jnp.arange(1, N_Q_HEADS + 1, dtype=F32) / N_Q_HEADS)
    tables = []
    for n in (0, 1, 2 if nb > 3 else None, nb - 1):
        if n is None:
            tables.append(tables[-1])
            continue
        key_pos = (n - 1) * BLOCK + ki
        ok = (dist <= WINDOW) & (key_pos >= PAD) & (key_pos < lp)
        alibi = -slopes[:, None, None] * dist.astype(F32)[None] * LOG2E
        tables.append(jnp.where(ok[None], alibi, NEG))
    return jnp.stack(tables)


def _attn_kernel(q_ref, kp_ref, kc_ref, kn_ref, vp_ref, vc_ref, vn_ref, bias_ref, sink_ref, o_ref,
                 k_scr, v_scr, s_scr, m_scr, p_scr):
    low_kv = lax.broadcasted_iota(jnp.int32, (3 * BLOCK, LANES), 1) < HEAD_DIM
    low_q = lax.broadcasted_iota(jnp.int32, (BLOCK, LANES), 1) < HEAD_DIM
    heads = _pair_heads()
    n_hp = N_KV_HEADS // 2
    for hp in range(n_hp):
        cols = slice(hp * LANES, (hp + 1) * LANES)
        kpair = jnp.concatenate([kp_ref[:, cols], kc_ref[:, cols], kn_ref[:, cols]], axis=0)
        vpair = jnp.concatenate([vp_ref[:, cols], vc_ref[:, cols], vn_ref[:, cols]], axis=0)
        zero = jnp.zeros_like(kpair)
        one = jnp.ones_like(vpair)
        k_scr[hp, 0] = jnp.where(low_kv, kpair, zero)
        k_scr[hp, 1] = jnp.where(low_kv, zero, kpair)
        v_scr[hp, 0] = jnp.where(low_kv, vpair, one)
        v_scr[hp, 1] = jnp.where(low_kv, one, vpair)
    tiles = [(hp, g, part) for hp in range(n_hp) for g in range(GQ) for part in range(2)]
    for t, (hp, g, part) in enumerate(tiles):
        pair = hp * GQ + g
        qp = q_ref[:, pair * LANES:(pair + 1) * LANES]
        s = lax.dot_general(qp, k_scr[hp, part], (((1,), (1,)), ((), ())), preferred_element_type=F32)
        s_scr[t] = s + bias_ref[heads[pair][part]]
    for t, (hp, g, part) in enumerate(tiles):
        sink = sink_ref[0, heads[hp * GQ + g][part]]
        m = jnp.maximum(jnp.max(s_scr[t], axis=-1, keepdims=True), sink)
        m_scr[t] = jnp.broadcast_to(m, (BLOCK, LANES))
    for t in range(len(tiles)):
        m = m_scr[t]
        for c in range(3):
            cs = slice(c * BLOCK, (c + 1) * BLOCK)
            p_scr[t, :, cs] = jnp.exp2(s_scr[t, :, cs] - m).astype(BF16)
    for t, (hp, g, part) in enumerate(tiles):
        pair = hp * GQ + g
        sink = sink_ref[0, heads[pair][part]]
        oe = jnp.dot(p_scr[t], v_scr[hp, part], preferred_element_type=F32)
        o = oe / (pltpu.roll(oe, HEAD_DIM, 1) + jnp.exp2(sink - m_scr[t]))
        if part == 0:
            first = o
        else:
            o_ref[:, pair * LANES:(pair + 1) * LANES] = jnp.where(low_q, first, o).astype(o_ref.dtype)


def _attention(q, k, v, sink, bsz, lp):
    nb = lp // BLOCK
    dq = N_Q_HEADS * HEAD_DIM
    dkv = N_KV_HEADS * HEAD_DIM
    q3, k3, v3 = q.reshape(bsz, lp, dq), k.reshape(bsz, lp, dkv), v.reshape(bsz, lp, dkv)
    sink2 = (sink.astype(F32) * LOG2E).reshape(1, N_Q_HEADS)
    kv_prev = pl.BlockSpec((None, BLOCK, dkv), lambda b, n: (b, jnp.maximum(n - 1, 0), 0))
    kv_cur = pl.BlockSpec((None, BLOCK, dkv), lambda b, n: (b, n, 0))
    kv_next = pl.BlockSpec((None, BLOCK, dkv), lambda b, n: (b, jnp.minimum(n + 1, nb - 1), 0))

    def bias_case(b, n):
        return (jnp.where(n == 0, 0, jnp.where(n == 1, 1, jnp.where(n == nb - 1, 3, 2))), 0, 0, 0)
    out = pl.pallas_call(
        _attn_kernel,
        grid=(bsz, nb),
        in_specs=[pl.BlockSpec((None, BLOCK, dq), lambda b, n: (b, n, 0)),
                  kv_prev, kv_cur, kv_next, kv_prev, kv_cur, kv_next,
                  pl.BlockSpec((None, N_Q_HEADS, BLOCK, 3 * BLOCK), bias_case),
                  pl.BlockSpec(memory_space=pltpu.SMEM)],
        out_specs=pl.BlockSpec((None, BLOCK, dq), lambda b, n: (b, n, 0)),
        out_shape=jax.ShapeDtypeStruct((bsz, lp, dq), BF16),
        scratch_shapes=[pltpu.VMEM((N_KV_HEADS // 2, 2, 3 * BLOCK, LANES), BF16),
                        pltpu.VMEM((N_KV_HEADS // 2, 2, 3 * BLOCK, LANES), BF16),
                        pltpu.VMEM((N_Q_HEADS, BLOCK, 3 * BLOCK), F32),
                        pltpu.VMEM((N_Q_HEADS, BLOCK, LANES), F32),
                        pltpu.VMEM((N_Q_HEADS, BLOCK, 3 * BLOCK), BF16)],
        compiler_params=_cparams(2),
        name="swa",
    )(q3, k3, k3, k3, v3, v3, v3, _attn_bias(lp), sink2)
    return out.reshape(bsz * lp, dq)


def _ffn_kernel(*refs, lp, tm, n_tiles, final, n_cur):
    hp_ref, hc_refs, hn_ref = refs[0], refs[1:1 + n_cur], refs[1 + n_cur]
    refs = refs[2 + n_cur:]
    i = pl.program_id(1 if final else 0)
    n = tm + 2 * FFN_HALO
    hp, hn = hp_ref[...], hn_ref[...]
    hc = jnp.concatenate([r[...] for r in hc_refs], axis=0) if n_cur > 1 else hc_refs[0][...]
    if final:
        op_ref, oc_refs, on_ref, wo_ref = refs[0], refs[1:1 + n_cur], refs[1 + n_cur], refs[2 + n_cur]
        refs, oext = refs[3 + n_cur:-1], refs[-1]
        oext[0:FFN_HALO, :] = op_ref[...]
        for k, r in enumerate(oc_refs):
            oext[FFN_HALO + k * BLOCK:FFN_HALO + (k + 1) * BLOCK, :] = r[...]
        oext[FFN_HALO + tm:n, :] = on_ref[...]
        proj = jnp.dot(oext[...], wo_ref[...], preferred_element_type=F32)
        hp, hc, hn = hp + proj[0:FFN_HALO], hc + proj[FFN_HALO:FFN_HALO + tm], hn + proj[FFN_HALO + tm:n]
    g_ref, wup_ref, cw_ref, cb_ref, wdn_ref, gf_ref, o_ref, nat_scr, hx_scr, act_scr = refs
    g = g_ref[...]
    nv = n // SUBLANES
    nslab = D_MODEL // LANES

    def put(r0, x):
        for l in range(nslab):
            nat_scr[l, r0:r0 + x.shape[0], :] = x[:, l * LANES:(l + 1) * LANES]
    keep = jnp.where(i < n_tiles - 1, 1.0, 0.0)
    put(0, _rmsnorm(hp, g))
    put(FFN_HALO, _rmsnorm(hc, g))
    put(FFN_HALO + tm, _rmsnorm(hn, g) * keep)

    def to_strided(jp, _):
        r = pl.ds(pl.multiple_of(jp * BF16_ROWS, BF16_ROWS), BF16_ROWS)
        for l in range(nslab):
            two = jnp.concatenate([nat_scr[l, pl.ds(2 * jp, SUBLANES, stride=nv), :],
                                   nat_scr[l, pl.ds(2 * jp + 1, SUBLANES, stride=nv), :]], axis=0)
            hx_scr[r, l * LANES:(l + 1) * LANES] = two.astype(BF16)
        return 0
    lax.fori_loop(0, nv // 2, to_strided, 0)

    def conv(u, cols):
        prev = jnp.concatenate([pltpu.roll(u[n - SUBLANES:], 1, 0), u[:n - SUBLANES]], axis=0)
        nxt = jnp.concatenate([u[SUBLANES:], pltpu.roll(u[:SUBLANES], SUBLANES - 1, 0)], axis=0)
        return (cw_ref[0:1, cols] * prev + cw_ref[1:2, cols] * u + cw_ref[2:3, cols] * nxt) + cb_ref[0:1, cols]

    for c in range(D_FF // FFN_TF):
        ca = slice(c * FFN_TF, (c + 1) * FFN_TF)
        cg = slice(D_FF + c * FFN_TF, D_FF + (c + 1) * FFN_TF)
        ua = jnp.dot(hx_scr[...], wup_ref[:, ca], preferred_element_type=F32)
        ug = jnp.dot(hx_scr[...], wup_ref[:, cg], preferred_element_type=F32)
        act_scr[:, ca] = (_gelu(conv(ug, cg)) * conv(ua, ca)).astype(BF16)

    ys = jnp.dot(act_scr[...], wdn_ref[...], preferred_element_type=F32)
    for j in range(nv):
        for l in range(nslab):
            nat_scr[l, pl.ds(j, SUBLANES, stride=nv), :] = ys[j * SUBLANES:(j + 1) * SUBLANES,
                                                              l * LANES:(l + 1) * LANES]
    y = hc + jnp.concatenate([nat_scr[l, FFN_HALO:FFN_HALO + tm, :] for l in range(nslab)], axis=1)
    if final:
        o_ref[...] = _rmsnorm(y, gf_ref[...])
    else:
        o_ref[...] = jnp.where(_valid_rows(i, tm, lp, y.shape), y, 0.0)


def _conv_ffn(h2, g, w_up, conv_w, conv_b, w_down, g_final, bsz, lp, final, o=None, w_o=None):
    rows, d = h2.shape
    const = dict(pipeline_mode=pl.Buffered(1))
    cw = jnp.pad(conv_w.astype(F32), ((0, SUBLANES - CONV_F), (0, 0)))
    if final:
        seq = lp - FRONT
        tm = _row_tile(seq, seq, mult=BLOCK)
        n_tiles, n_cur = seq // tm, tm // BLOCK
        grid = (bsz, n_tiles)
        last = lp // FFN_HALO - 1
        h_in = h2.reshape(bsz, lp, d)
        prev = pl.BlockSpec((None, FFN_HALO, d), lambda b, j: (b, (FRONT + j * tm) // FFN_HALO - 1, 0))
        cur = [pl.BlockSpec((None, BLOCK, d), lambda b, j, k=k: (b, 1 + j * n_cur + k, 0)) for k in range(n_cur)]
        nxt = pl.BlockSpec((None, FFN_HALO, d),
                           lambda b, j: (b, jnp.minimum((FRONT + (j + 1) * tm) // FFN_HALO, last), 0))
        out_spec = pl.BlockSpec((None, tm, d), lambda b, j: (b, j, 0))
        out_shape = jax.ShapeDtypeStruct((bsz, seq, d), F32)

        def whole(shape):
            return pl.BlockSpec(shape, lambda b, j: (0,) * len(shape))
        extra_specs = [prev] + cur + [nxt, pl.BlockSpec(w_o.shape, whole(w_o.shape).index_map, **const)]
        extra_ops = [o.reshape(bsz, lp, d)] * (2 + n_cur) + [w_o.astype(BF16)]
        extra_scratch = [pltpu.VMEM((tm + 2 * FFN_HALO, d), BF16)]
    else:
        tm = _row_tile(rows, lp, cap=FFN_TM)
        n_tiles, n_cur = rows // tm, 1
        grid = (n_tiles,)
        hb = tm // FFN_HALO
        last = rows // FFN_HALO - 1
        h_in = h2
        prev = pl.BlockSpec((FFN_HALO, d), lambda i: (jnp.maximum(i * hb - 1, 0), 0))
        cur = [pl.BlockSpec((tm, d), lambda i: (i, 0))]
        nxt = pl.BlockSpec((FFN_HALO, d), lambda i: (jnp.minimum((i + 1) * hb, last), 0))
        out_spec = pl.BlockSpec((tm, d), lambda i: (i, 0))
        out_shape = jax.ShapeDtypeStruct((rows, d), F32)

        def whole(shape):
            return pl.BlockSpec(shape, lambda i: (0,) * len(shape))
        extra_specs, extra_ops, extra_scratch = [], [], []
    weights = [pl.BlockSpec((1, d), whole((1, d)).index_map),
               pl.BlockSpec((d, 2 * D_FF), whole((d, 2 * D_FF)).index_map, **const),
               whole((SUBLANES, 2 * D_FF)), whole((1, 2 * D_FF)),
               pl.BlockSpec((D_FF, d), whole((D_FF, d)).index_map, **const),
               whole((1, d))]
    return pl.pallas_call(
        functools.partial(_ffn_kernel, lp=lp, tm=tm, n_tiles=n_tiles, final=final, n_cur=n_cur),
        grid=grid,
        in_specs=[prev] + cur + [nxt] + extra_specs + weights,
        out_specs=out_spec,
        out_shape=out_shape,
        scratch_shapes=[pltpu.VMEM((d // LANES, tm + 2 * FFN_HALO, LANES), F32),
                        pltpu.VMEM((tm + 2 * FFN_HALO, d), BF16),
                        pltpu.VMEM((tm + 2 * FFN_HALO, D_FF), BF16)] + extra_scratch,
        compiler_params=_cparams(len(grid)),
        name="conv_ffn",
    )(*([h_in] * (2 + n_cur)), *extra_ops, g.astype(F32).reshape(1, d), w_up.astype(BF16), cw,
      conv_b.astype(F32).reshape(1, 2 * D_FF), w_down.astype(BF16), g_final.astype(F32).reshape(1, d))


def _trunk(x, prm):
    bsz, seq, d = x.shape
    lp = seq + FRONT
    assert seq % BLOCK == 0 and d == D_MODEL

    h, u_a, x_b, g_b = _in_proj(x, prm['meta_tokens'], prm['norm_mix_g'][0], prm['w_in_ab'][0].astype(BF16), lp)
    s5w = _s5_weights(prm['s5_lambda_re'][0], prm['s5_lambda_im'][0], prm['s5_log_dt'][0],
                      prm['s5_b_re'][0], prm['s5_b_im'][0], prm['s5_c_re'][0], prm['s5_c_im'][0], prm['s5_d'][0])
    y_a = _s5_mixer(u_a, s5w, bsz, lp)
    lruw = _lru_weights(prm['lru_conv_w'][0], prm['lru_conv_b'][0], prm['lru_w_r'][0], prm['lru_b_r'][0],
                        prm['lru_w_i'][0], prm['lru_b_i'][0], prm['lru_lambda'][0])
    hf, hb = _lru_mixer(x_b, lruw, bsz, lp)
    h = _ab_out(y_a, hf, hb, g_b, h, prm['w_glu'][0], prm['b_glu'][0], prm['w_out_ab'][0], lp)
    h = _conv_ffn(h, prm['norm_ffn_g'][0], prm['w_up'][0], prm['ffn_conv_w'][0], prm['ffn_conv_b'][0],
                  prm['w_down'][0], prm['final_norm_g'], bsz, lp, final=False)

    dq, dkv = N_Q_HEADS * HEAD_DIM, N_KV_HEADS * HEAD_DIM
    head_order = jnp.asarray([h_ for pair in _pair_heads() for h_ in pair])
    col_order = (head_order[:, None] * HEAD_DIM + jnp.arange(HEAD_DIM)[None, :]).reshape(-1)
    w_qkv = prm['w_qkv'][0]
    w_qkv = jnp.concatenate([w_qkv[:, :dq][:, col_order], w_qkv[:, dq:]], axis=1).astype(BF16)
    q, k, v = _norm_mm(h, prm['norm_mix_g'][1], w_qkv, (dq, dkv, dkv),
                       (LOG2E / math.sqrt(HEAD_DIM), 1.0, 1.0), lp)
    o = _attention(q, k, v, prm['attn_sink'][0], bsz, lp)
    return _conv_ffn(h, prm['norm_ffn_g'][1], prm['w_up'][1], prm['ffn_conv_w'][1], prm['ffn_conv_b'][1],
                     prm['w_down'][1], prm['final_norm_g'], bsz, lp, final=True, o=o, w_o=prm['w_o'][0][col_order])


def kernel(x_prompt, x_sample, meta_tokens, norm_mix_g, norm_ffn_g, final_norm_g, w_in_ab, s5_lambda_re, s5_lambda_im, s5_log_dt, s5_b_re, s5_b_im, s5_c_re, s5_c_im, s5_d, w_glu, b_glu, lru_conv_w, lru_conv_b, lru_w_r, lru_b_r, lru_w_i, lru_b_i, lru_lambda, w_out_ab, w_qkv, w_o, attn_sink, w_up, ffn_conv_w, ffn_conv_b, w_down):
    prm = dict(meta_tokens=meta_tokens, norm_mix_g=norm_mix_g, norm_ffn_g=norm_ffn_g, final_norm_g=final_norm_g,
               w_in_ab=w_in_ab, s5_lambda_re=s5_lambda_re, s5_lambda_im=s5_lambda_im, s5_log_dt=s5_log_dt,
               s5_b_re=s5_b_re, s5_b_im=s5_b_im, s5_c_re=s5_c_re, s5_c_im=s5_c_im, s5_d=s5_d, w_glu=w_glu,
               b_glu=b_glu, lru_conv_w=lru_conv_w, lru_conv_b=lru_conv_b, lru_w_r=lru_w_r, lru_b_r=lru_b_r,
               lru_w_i=lru_w_i, lru_b_i=lru_b_i, lru_lambda=lru_lambda, w_out_ab=w_out_ab, w_qkv=w_qkv,
               w_o=w_o, attn_sink=attn_sink, w_up=w_up, ffn_conv_w=ffn_conv_w, ffn_conv_b=ffn_conv_b,
               w_down=w_down)
    return (_trunk(x_prompt, prm), _trunk(x_sample, prm))
```

```python
import functools
import math

import jax
import jax.numpy as jnp
from jax import lax
from jax.experimental import pallas as pl
from jax.experimental.pallas import tpu as pltpu

D_MODEL = 1024
N_META = 16
D_A = 512
S5_GROUP = 16
G_A = D_A // S5_GROUP
S5_STATE = 64
D_B = 512
H_B = 8
BW_B = D_B // H_B
LRU_C = 8.0
CONV_B = 4
CONV_B_LEFT = 2
HEAD_DIM = 64
N_Q_HEADS = 16
N_KV_HEADS = 4
GQ = N_Q_HEADS // N_KV_HEADS
WINDOW = 128
BLOCK = 128
D_FF = 2816
CONV_F = 3
EPS = 1e-6
NEG = -1e30

PAD = BLOCK - N_META
FRONT = PAD + N_META
S5_T = 16
S5_W = S5_T * S5_GROUP
LANES = 128
SUBLANES = 8
BF16_ROWS = 16
FFN_HALO = BF16_ROWS
FFN_TF = 256
FFN_TM = 1024
LRU_CB = LANES
VMEM_LIMIT = 56 * 1024 * 1024

F32 = jnp.float32
BF16 = jnp.bfloat16


def _cparams(n_axes):
    return pltpu.CompilerParams(
        dimension_semantics=("arbitrary",) * n_axes, vmem_limit_bytes=VMEM_LIMIT)


def _row_tile(rows, lp, cap=512, mult=BF16_ROWS):
    best = None
    t = mult
    while t <= min(cap, lp):
        if rows % t == 0:
            best = t
        t += mult
    assert best is not None, (rows, lp, cap, mult)
    return best


def _lru_tile(lp):
    q = lp // 32
    assert lp % 32 == 0
    odd = q
    while odd % 2 == 0:
        odd //= 2
    best = 1
    for d in range(1, odd + 1, 2):
        if odd % d == 0 and 32 * d <= 640:
            best = d
    return 32 * best


def _rmsnorm(x, g):
    ms = jnp.mean(x * x, axis=-1, keepdims=True)
    return x * lax.rsqrt(ms + EPS) * g


def _gelu(x):
    k1 = -2.0 * math.sqrt(2.0 / math.pi) * math.log2(math.e)
    return x / (1.0 + jnp.exp2(x * (k1 + (k1 * 0.044715) * (x * x))))


def _sigmoid(x):
    return 1.0 / (1.0 + jnp.exp(-x))


def _valid_rows(tile_idx, tm, lp, shape):
    pos = lax.rem(tile_idx * tm, lp) + lax.broadcasted_iota(jnp.int32, shape, 0)
    pos = jnp.where(pos >= lp, pos - lp, pos)
    return pos >= PAD


def _norm_mm_kernel(h_ref, g_ref, w_ref, *o_refs, scales):
    hn = _rmsnorm(h_ref[...], g_ref[...]).astype(BF16)
    off = 0
    for o_ref, sc in zip(o_refs, scales):
        n = o_ref.shape[-1]
        z = jnp.dot(hn, w_ref[:, off:off + n], preferred_element_type=F32)
        if sc != 1.0:
            z = z * sc
        o_ref[...] = z.astype(o_ref.dtype)
        off += n


def _norm_mm(h2, g, w, splits, scales, lp):
    rows, d = h2.shape
    tm = _row_tile(rows, lp)
    n = w.shape[1]
    return pl.pallas_call(
        functools.partial(_norm_mm_kernel, scales=scales),
        grid=(rows // tm,),
        in_specs=[pl.BlockSpec((tm, d), lambda i: (i, 0)),
                  pl.BlockSpec((1, d), lambda i: (0, 0)),
                  pl.BlockSpec((d, n), lambda i: (0, 0))],
        out_specs=[pl.BlockSpec((tm, s), lambda i: (i, 0)) for s in splits],
        out_shape=[jax.ShapeDtypeStruct((rows, s), BF16) for s in splits],
        compiler_params=_cparams(1),
        name="norm_mm",
    )(h2, g.reshape(1, d), w)


UNIT = S5_T * BF16_ROWS
GRP_PER_VREG = LANES // S5_GROUP


def _unit_perm():
    r = jnp.arange(UNIT)
    src = (r % BF16_ROWS) * S5_T + r // BF16_ROWS
    return (src[:, None] == jnp.arange(UNIT)[None, :]).astype(BF16)


def _tokens_to_chunks(zp, store):
    grp = lax.broadcasted_iota(jnp.int32, (BF16_ROWS, LANES), 1) // S5_GROUP
    for j in range(D_A // LANES):
        v = [zp[s * BF16_ROWS:(s + 1) * BF16_ROWS, j * LANES:(j + 1) * LANES] for s in range(S5_T)]
        for gl in range(GRP_PER_VREG):
            for half in range(S5_T // GRP_PER_VREG):
                acc = None
                for p in range(GRP_PER_VREG):
                    src = v[half * GRP_PER_VREG + p]
                    shift = S5_GROUP * ((p - gl) % GRP_PER_VREG)
                    val = src if shift == 0 else pltpu.roll(src, shift, 1)
                    acc = val if acc is None else jnp.where(grp == p, val, acc)
                store(j * GRP_PER_VREG + gl, half, acc)


def _chunks_to_tokens(load):
    grp = lax.broadcasted_iota(jnp.int32, (BF16_ROWS, LANES), 1) // S5_GROUP
    cols = []
    for j in range(D_A // LANES):
        rows = []
        for half in range(S5_T // GRP_PER_VREG):
            y = [load(j * GRP_PER_VREG + gl, half) for gl in range(GRP_PER_VREG)]
            for p in range(GRP_PER_VREG):
                acc = None
                for gl in range(GRP_PER_VREG):
                    shift = S5_GROUP * ((gl - p) % GRP_PER_VREG)
                    val = y[gl] if shift == 0 else pltpu.roll(y[gl], shift, 1)
                    acc = val if acc is None else jnp.where(grp == gl, val, acc)
                rows.append(acc)
        cols.append(jnp.concatenate(rows, axis=0))
    return jnp.concatenate(cols, axis=1)


def _in_proj_kernel(*refs, tm, nb):
    n_cur = tm // BLOCK
    x_refs = refs[:n_cur]
    meta_ref, g_ref, w_ref, perm_ref, h_ref, u_ref, xb_ref, gb_ref = refs[n_cur:]
    first = pl.program_id(0) * n_cur
    blocks = [jnp.where(lax.rem(first + k, nb) == 0, meta_ref[...], x_refs[k][...]) for k in range(n_cur)]
    h = jnp.concatenate(blocks, axis=0)
    h_ref[...] = h
    hn = _rmsnorm(h, g_ref[...]).astype(BF16)
    xb_ref[...] = jnp.dot(hn, w_ref[:, D_A:D_A + D_B], preferred_element_type=F32).astype(BF16)
    gb_ref[...] = jnp.dot(hn, w_ref[:, D_A + D_B:], preferred_element_type=F32).astype(BF16)
    za = jnp.dot(hn, w_ref[:, 0:D_A], preferred_element_type=F32).astype(BF16)
    for q in range(tm // UNIT):
        zp = jnp.dot(perm_ref[...], za[q * UNIT:(q + 1) * UNIT], preferred_element_type=F32)

        def store(g, half, x, q=q):
            u_ref[g, q * BF16_ROWS:(q + 1) * BF16_ROWS, half * LANES:(half + 1) * LANES] = x.astype(BF16)
        _tokens_to_chunks(zp, store)


def _in_proj(x, meta, g, w, lp):
    bsz, seq, d = x.shape
    rows = bsz * lp
    nb = lp // BLOCK
    tm = _row_tile(rows, lp, mult=UNIT)
    n_cur = tm // BLOCK
    n = w.shape[1]
    meta_blk = jnp.pad(meta.astype(F32), ((PAD, 0), (0, 0)))
    x_specs = [pl.BlockSpec((None, BLOCK, d),
                            lambda i, k=k: ((i * n_cur + k) // nb, jnp.maximum((i * n_cur + k) % nb - 1, 0), 0))
               for k in range(n_cur)]
    return pl.pallas_call(
        functools.partial(_in_proj_kernel, tm=tm, nb=nb),
        grid=(rows // tm,),
        in_specs=x_specs + [pl.BlockSpec((BLOCK, d), lambda i: (0, 0)),
                            pl.BlockSpec((1, d), lambda i: (0, 0)),
                            pl.BlockSpec((d, n), lambda i: (0, 0)),
                            pl.BlockSpec((UNIT, UNIT), lambda i: (0, 0))],
        out_specs=[pl.BlockSpec((tm, d), lambda i: (i, 0)),
                   pl.BlockSpec((G_A, tm // S5_T, S5_W), lambda i: (0, i, 0)),
                   pl.BlockSpec((tm, D_B), lambda i: (i, 0)),
                   pl.BlockSpec((tm, D_B), lambda i: (i, 0))],
        out_shape=[jax.ShapeDtypeStruct((rows, d), F32),
                   jax.ShapeDtypeStruct((G_A, rows // S5_T, S5_W), BF16),
                   jax.ShapeDtypeStruct((rows, D_B), BF16),
                   jax.ShapeDtypeStruct((rows, D_B), BF16)],
        compiler_params=_cparams(1),
        name="in_proj",
    )(*([x.astype(F32)] * n_cur), meta_blk, g.reshape(1, d), w, _unit_perm())


def _cmul(ar, ai, br, bi):
    return ar * br - ai * bi, ar * bi + ai * br


def _s5_weights(lam_re, lam_im, log_dt, b_re, b_im, c_re, c_im, d_skip):
    hp = lax.Precision.HIGHEST
    t = S5_T
    lr = jnp.minimum(lam_re.astype(F32), -1e-4)
    li = lam_im.astype(F32)
    dt = jnp.exp(log_dt.astype(F32))[..., None]
    mag = jnp.exp(lr * dt)
    lbr, lbi = mag * jnp.cos(li * dt), mag * jnp.sin(li * dt)
    den = lr * lr + li * li
    xr, xi = lbr - 1.0, lbi
    fr, fi = (xr * lr + xi * li) / den, (xi * lr - xr * li) / den
    bbr, bbi = _cmul(fr[..., None], fi[..., None], b_re.astype(F32), b_im.astype(F32))
    pr, pi = [jnp.ones_like(lbr)], [jnp.zeros_like(lbi)]
    for _ in range(t):
        nr, ni = _cmul(pr[-1], pi[-1], lbr, lbi)
        pr.append(nr)
        pi.append(ni)
    pwr, pwi = jnp.stack(pr, -1), jnp.stack(pi, -1)
    cr, ci = c_re.astype(F32), c_im.astype(F32)
    wr, wi = _cmul(cr[..., None], ci[..., None], pwr[:, :, None], pwi[:, :, None])
    kk = jnp.einsum('dgonk,dgni->dgkio', jnp.concatenate([wr[..., :t], -wi[..., :t]], axis=3),
                    jnp.concatenate([bbr, bbi], axis=2), precision=hp)
    eye = jnp.eye(S5_GROUP, dtype=F32) * d_skip.astype(F32).reshape(G_A, 1, S5_GROUP)
    k_all = jnp.concatenate([kk[1][:, 1:][:, ::-1],
                             (kk[0][:, 0] + kk[1][:, 0] + eye)[:, None],
                             kk[0][:, 1:]], axis=1)
    sidx = jnp.arange(t)
    lag = sidx[None, :] - sidx[:, None] + (t - 1)
    m = k_all[:, lag]
    m = m.transpose(0, 1, 3, 2, 4).reshape(G_A, S5_W, S5_W)
    pfr, pfi = _cmul(pwr[0][..., :t][..., ::-1][..., None], pwi[0][..., :t][..., ::-1][..., None],
                     bbr[0][:, :, None, :], bbi[0][:, :, None, :])
    pbr, pbi = _cmul(pwr[1][..., :t][..., None], pwi[1][..., :t][..., None],
                     bbr[1][:, :, None, :], bbi[1][:, :, None, :])

    def p_cols(x):
        x = x.transpose(0, 2, 3, 1).reshape(G_A, S5_W, S5_STATE)
        return jnp.pad(x, ((0, 0), (0, 0), (0, LANES - S5_STATE)))
    p = jnp.concatenate([p_cols(pfr), p_cols(pfi), p_cols(pbr), p_cols(pbi)], axis=-1)

    def q_rows(x):
        x = x.transpose(0, 2, 3, 1).reshape(G_A, S5_STATE, S5_W)
        return jnp.pad(x, ((0, 0), (0, LANES - S5_STATE), (0, 0)))
    q = jnp.concatenate([q_rows(wr[0][..., 1:]), q_rows(-wi[0][..., 1:]),
                         q_rows(wr[1][..., 1:][..., ::-1]), q_rows(-wi[1][..., 1:][..., ::-1])], axis=1)

    def c_row(x):
        return jnp.pad(x, ((0, 0), (0, LANES - S5_STATE)))[:, None]
    coef = jnp.concatenate([c_row(pwr[0][..., t]), c_row(pwi[0][..., t]),
                            c_row(pwr[1][..., t]), c_row(pwi[1][..., t]),
                            jnp.zeros((G_A, 4, LANES), F32)], axis=1)
    return m.astype(BF16), p.astype(BF16), q.astype(BF16), coef


def _s5_kernel(u_ref, m_ref, p_ref, q_ref, coef_ref, y_ref, s_scr, x0_scr, *, n_chunks, bsz, rb):
    rows = n_chunks * bsz
    ncomp = 4
    for k in range(rows // rb):
        sl = slice(k * rb, (k + 1) * rb)
        s = jnp.dot(u_ref[sl, :], p_ref[...], preferred_element_type=F32)
        for c in range(ncomp):
            s_scr[c, sl, :] = s[:, c * LANES:(c + 1) * LANES]

    shp = (bsz, LANES)
    lfr = jnp.broadcast_to(coef_ref[0:1, :], shp)
    lfi = jnp.broadcast_to(coef_ref[1:2, :], shp)
    lbr = jnp.broadcast_to(coef_ref[2:3, :], shp)
    lbi = jnp.broadcast_to(coef_ref[3:4, :], shp)

    def step(j, carry):
        xfr, xfi, xbr, xbi = carry
        rf = pl.ds(j, bsz, stride=n_chunks)
        rbk = pl.ds(n_chunks - 1 - j, bsz, stride=n_chunks)
        x0_scr[0, rf, :] = xfr
        x0_scr[1, rf, :] = xfi
        x0_scr[2, rbk, :] = xbr
        x0_scr[3, rbk, :] = xbi
        nfr = lfr * xfr - lfi * xfi + s_scr[0, rf, :]
        nfi = lfr * xfi + lfi * xfr + s_scr[1, rf, :]
        nbr = lbr * xbr - lbi * xbi + s_scr[2, rbk, :]
        nbi = lbr * xbi + lbi * xbr + s_scr[3, rbk, :]
        return nfr, nfi, nbr, nbi

    z = jnp.zeros(shp, F32)
    lax.fori_loop(0, n_chunks, step, (z, z, z, z), unroll=4 if n_chunks % 4 == 0 else 1)

    for k in range(rows // rb):
        sl = slice(k * rb, (k + 1) * rb)
        x0 = jnp.concatenate([x0_scr[c, sl, :] for c in range(ncomp)], axis=1).astype(BF16)
        y = jnp.dot(u_ref[sl, :], m_ref[...], preferred_element_type=F32)
        y = y + jnp.dot(x0, q_ref[...], preferred_element_type=F32)
        y_ref[sl, :] = y.astype(y_ref.dtype)


def _s5_mixer(ut, weights, bsz, lp):
    m, p, q, coef = weights
    n_chunks = lp // S5_T
    rows = n_chunks * bsz
    rb = _row_tile(rows, rows, cap=640)
    return pl.pallas_call(
        functools.partial(_s5_kernel, n_chunks=n_chunks, bsz=bsz, rb=rb),
        grid=(G_A,),
        in_specs=[pl.BlockSpec((None, rows, S5_W), lambda g: (g, 0, 0)),
                  pl.BlockSpec((None, S5_W, S5_W), lambda g: (g, 0, 0)),
                  pl.BlockSpec((None, S5_W, 4 * LANES), lambda g: (g, 0, 0)),
                  pl.BlockSpec((None, 4 * LANES, S5_W), lambda g: (g, 0, 0)),
                  pl.BlockSpec((None, SUBLANES, LANES), lambda g: (g, 0, 0))],
        out_specs=pl.BlockSpec((None, rows, S5_W), lambda g: (g, 0, 0)),
        out_shape=jax.ShapeDtypeStruct((G_A, rows, S5_W), BF16),
        scratch_shapes=[pltpu.VMEM((4, rows, LANES), F32), pltpu.VMEM((4, rows, LANES), F32)],
        compiler_params=_cparams(1),
        name="s5_chunks",
    )(ut, m, p, q, coef)


def _lru_weights(conv_w, conv_b, w_r, b_r, w_i, b_i, lam):
    ncb = D_B // LRU_CB
    hpb = LRU_CB // BW_B

    def blockdiag(w):
        w = w.astype(F32).reshape(2, ncb, hpb, BW_B, BW_B)
        out = jnp.zeros((2, ncb, hpb, BW_B, hpb, BW_B), F32)
        for k in range(hpb):
            out = out.at[:, :, k, :, k, :].set(w[:, :, k])
        return out.reshape(2, ncb, LRU_CB, LRU_CB)
    wg = jnp.concatenate([blockdiag(w_r), blockdiag(w_i)], axis=-1).astype(BF16)
    bg = jnp.concatenate([b_r.astype(F32).reshape(2, ncb, 1, LRU_CB),
                          b_i.astype(F32).reshape(2, ncb, 1, LRU_CB)], axis=-1)
    cl = (-LRU_C * LOG2E * jax.nn.softplus(-lam.astype(F32))).reshape(2, ncb, 1, LRU_CB)
    cw = jnp.pad(conv_w.astype(F32), ((0, SUBLANES - CONV_B), (0, 0)))
    cb = conv_b.astype(F32).reshape(1, D_B)
    return cw, cb, wg, bg, cl


LRU_HALO = BF16_ROWS
N_CB = D_B // LRU_CB


def _lru_kernel(fp_ref, fc_ref, fn_ref, bp_ref, bc_ref, bn_ref, cw_ref, cb_ref, wg_ref, bg_ref, cl_ref,
                hf_ref, hb_ref, xw, a_scr, b_scr, h_scr, init_scr, carry_scr, *, tt, n_tiles):
    t = pl.program_id(1)
    seg = tt // SUBLANES
    sb = tt // 4
    halo = LRU_HALO
    dirs = ((fp_ref, fc_ref, fn_ref, t), (bp_ref, bc_ref, bn_ref, n_tiles - 1 - t))

    @pl.when(t == 0)
    def _():
        carry_scr[...] = jnp.zeros_like(carry_scr)

    for d, (p_ref, c_ref, n_ref, ti) in enumerate(dirs):
        keep = jnp.where(ti < n_tiles - 1, 1.0, 0.0)
        xw[d, 0:halo, :] = p_ref[...].astype(F32)
        xw[d, halo:halo + tt, :] = c_ref[...].astype(F32)
        xw[d, halo + tt:2 * halo + tt, :] = n_ref[...].astype(F32) * keep

    def gates(k, _):
        n = sb + 2 * SUBLANES
        base = pl.multiple_of(k * sb, SUBLANES)
        rows = pl.ds(base, sb)
        for d in range(2):
            for cb in range(N_CB):
                lanes = slice(cb * LRU_CB, (cb + 1) * LRU_CB)
                win = xw[d, pl.ds(base + halo - SUBLANES, n), lanes]
                lo, hi = SUBLANES, SUBLANES + sb
                xc = (cw_ref[0:1, lanes] * pltpu.roll(win, 2, 0)[lo:hi]
                      + cw_ref[1:2, lanes] * pltpu.roll(win, 1, 0)[lo:hi]
                      + cw_ref[2:3, lanes] * win[lo:hi]
                      + cw_ref[3:4, lanes] * pltpu.roll(win, n - 1, 0)[lo:hi]) + cb_ref[0:1, lanes]
                z = jnp.dot(xc.astype(BF16), wg_ref[d, cb], preferred_element_type=F32) + bg_ref[d, cb]
                r = _sigmoid(z[:, :LRU_CB])
                ig = _sigmoid(z[:, LRU_CB:])
                a = jnp.exp2(cl_ref[d, cb] * r)
                y = 1.0 - a * a
                root = jnp.where(y > 0.0, y * lax.rsqrt(y), 0.0)
                a_scr[d, cb, rows, :] = a
                b_scr[d, cb, rows, :] = root * (ig * xc)
        return 0
    lax.fori_loop(0, tt // sb, gates, 0)

    for k in range(-(-PAD // tt)):
        npad = min(tt, PAD - k * tt)

        @pl.when(t == k)
        def _(npad=npad):
            for cb in range(N_CB):
                b_scr[0, cb, 0:npad, :] = jnp.zeros((npad, LANES), F32)

    chains = [(d, cb) for d in range(2) for cb in range(N_CB)]

    def rows_of(d, i):
        return pl.ds(i if d == 0 else seg - 1 - i, SUBLANES, stride=seg)

    def p1(i, c):
        out = []
        for n, (d, cb) in enumerate(chains):
            h, acc = c[2 * n], c[2 * n + 1]
            a = a_scr[d, cb, rows_of(d, i), :]
            out += [a * h + b_scr[d, cb, rows_of(d, i), :], a * acc]
        return tuple(out)

    z8 = jnp.zeros((SUBLANES, LANES), F32)
    o8 = jnp.ones((SUBLANES, LANES), F32)
    ends = lax.fori_loop(0, seg, p1, (z8, o8) * len(chains))
    for n, (d, cb) in enumerate(chains):
        he, ae = ends[2 * n], ends[2 * n + 1]
        c = carry_scr[d, cb, 0:1, :]
        for s in (range(SUBLANES) if d == 0 else reversed(range(SUBLANES))):
            init_scr[d, cb, s:s + 1, :] = c
            c = he[s:s + 1] + ae[s:s + 1] * c
        carry_scr[d, cb, 0:1, :] = c

    def p2(i, c):
        out = []
        for n, (d, cb) in enumerate(chains):
            h = a_scr[d, cb, rows_of(d, i), :] * c[n] + b_scr[d, cb, rows_of(d, i), :]
            h_scr[d, cb, rows_of(d, i), :] = h
            out.append(h)
        return tuple(out)
    lax.fori_loop(0, seg, p2, tuple(init_scr[d, cb] for d, cb in chains))

    for d, o_ref in enumerate((hf_ref, hb_ref)):
        for cb in range(N_CB):
            o_ref[:, cb * LRU_CB:(cb + 1) * LRU_CB] = h_scr[d, cb].astype(o_ref.dtype)


def _lru_mixer(x_b, weights, bsz, lp):
    cw, cb, wg, bg, cl = weights
    tt = _lru_tile(lp)
    n_tiles = lp // tt
    hpt = tt // LRU_HALO
    last = lp // LRU_HALO - 1
    x3 = x_b.reshape(bsz, lp, D_B)

    def specs(tile_of):
        return [pl.BlockSpec((None, LRU_HALO, D_B), lambda b, t: (b, jnp.maximum(tile_of(t) * hpt - 1, 0), 0)),
                pl.BlockSpec((None, tt, D_B), lambda b, t: (b, tile_of(t), 0)),
                pl.BlockSpec((None, LRU_HALO, D_B), lambda b, t: (b, jnp.minimum((tile_of(t) + 1) * hpt, last), 0))]
    fwd = specs(lambda t: t)
    bwd = specs(lambda t: n_tiles - 1 - t)

    def whole(a):
        return pl.BlockSpec(a.shape, lambda b, t: (0,) * a.ndim)
    hf, hb = pl.pallas_call(
        functools.partial(_lru_kernel, tt=tt, n_tiles=n_tiles),
        grid=(bsz, n_tiles),
        in_specs=fwd + bwd + [whole(cw), whole(cb), whole(wg), whole(bg), whole(cl)],
        out_specs=[fwd[1], bwd[1]],
        out_shape=[jax.ShapeDtypeStruct((bsz, lp, D_B), BF16)] * 2,
        scratch_shapes=[pltpu.VMEM((2, tt + 2 * LRU_HALO, D_B), F32),
                        pltpu.VMEM((2, N_CB, tt, LANES), F32), pltpu.VMEM((2, N_CB, tt, LANES), F32),
                        pltpu.VMEM((2, N_CB, tt, LANES), F32),
                        pltpu.VMEM((2, N_CB, SUBLANES, LANES), F32),
                        pltpu.VMEM((2, N_CB, SUBLANES, LANES), F32)],
        compiler_params=_cparams(2),
        name="rglru",
    )(x3, x3, x3, x3, x3, x3, cw, cb, wg, bg, cl)
    return hf.reshape(bsz * lp, D_B), hb.reshape(bsz * lp, D_B)


def _ab_out_kernel(yt_ref, hf_ref, hb_ref, gb_ref, h_ref, permt_ref, wglu_ref, bglu_ref, wo_ref, o_ref, *, lp, tm):
    yb = ((hf_ref[...].astype(F32) + hb_ref[...].astype(F32)) * _gelu(gb_ref[...].astype(F32))).astype(BF16)
    units = []
    for q in range(tm // UNIT):
        def load(g, half, q=q):
            return yt_ref[g, q * BF16_ROWS:(q + 1) * BF16_ROWS, half * LANES:(half + 1) * LANES].astype(F32)
        ysc = _chunks_to_tokens(load).astype(BF16)
        units.append(jnp.dot(permt_ref[...], ysc, preferred_element_type=F32))
    ya = _gelu(jnp.concatenate(units, axis=0))
    gate = _sigmoid(jnp.dot(ya.astype(BF16), wglu_ref[...], preferred_element_type=F32) + bglu_ref[...])
    ya = (ya * gate).astype(BF16)
    acc = jnp.dot(ya, wo_ref[0:D_A, :], preferred_element_type=F32)
    acc = acc + jnp.dot(yb, wo_ref[D_A:, :], preferred_element_type=F32)
    valid = _valid_rows(pl.program_id(0), tm, lp, acc.shape)
    o_ref[...] = jnp.where(valid, h_ref[...] + acc, 0.0)


def _ab_out(yt, hf, hb, g_b, h2, w_glu, b_glu, w_out, lp):
    rows, d = h2.shape
    tm = _row_tile(rows, lp, mult=UNIT)
    return pl.pallas_call(
        functools.partial(_ab_out_kernel, lp=lp, tm=tm),
        grid=(rows // tm,),
        in_specs=[pl.BlockSpec((G_A, tm // S5_T, S5_W), lambda i: (0, i, 0)),
                  pl.BlockSpec((tm, D_B), lambda i: (i, 0)),
                  pl.BlockSpec((tm, D_B), lambda i: (i, 0)),
                  pl.BlockSpec((tm, D_B), lambda i: (i, 0)),
                  pl.BlockSpec((tm, d), lambda i: (i, 0)),
                  pl.BlockSpec((UNIT, UNIT), lambda i: (0, 0)),
                  pl.BlockSpec((D_A, D_A), lambda i: (0, 0)),
                  pl.BlockSpec((1, D_A), lambda i: (0, 0)),
                  pl.BlockSpec((D_A + D_B, d), lambda i: (0, 0))],
        out_specs=pl.BlockSpec((tm, d), lambda i: (i, 0)),
        out_shape=jax.ShapeDtypeStruct((rows, d), F32),
        compiler_params=_cparams(1),
        name="ab_out",
    )(yt, hf, hb, g_b, h2, _unit_perm().T, w_glu.astype(BF16), b_glu.astype(F32).reshape(1, D_A),
      w_out.astype(BF16))


LOG2E = math.log2(math.e)
N_PAIRS = N_Q_HEADS // 2


def _pair_heads():
    return [((2 * hp) * GQ + g, (2 * hp + 1) * GQ + g) for hp in range(N_KV_HEADS // 2) for g in range(GQ)]


def _attn_bias(lp):
    nb = lp // BLOCK
    assert nb >= 3
    qi = jnp.arange(BLOCK)[:, None]
    ki = jnp.arange(3 * BLOCK)[None, :]
    dist = jnp.abs(qi + BLOCK - ki)
    slopes = 2.0 ** (-8.0 * jnp.arange(1, N_Q_HEADS + 1, dtype=F32) / N_Q_HEADS)
    tables = []
    for n in (0, 1, 2 if nb > 3 else None, nb - 1):
        if n is None:
            tables.append(tables[-1])
            continue
        key_pos = (n - 1) * BLOCK + ki
        ok = (dist <= WINDOW) & (key_pos >= PAD) & (key_pos < lp)
        alibi = -slopes[:, None, None] * dist.astype(F32)[None] * LOG2E
        tables.append(jnp.where(ok[None], alibi, NEG))
    return jnp.stack(tables)


def _attn_kernel(q_ref, kp_ref, kc_ref, kn_ref, vp_ref, vc_ref, vn_ref, bias_ref, sink_ref, o_ref,
                 k_scr, v_scr, s_scr, m_scr, p_scr):
    low_kv = lax.broadcasted_iota(jnp.int32, (3 * BLOCK, LANES), 1) < HEAD_DIM
    low_q = lax.broadcasted_iota(jnp.int32, (BLOCK, LANES), 1) < HEAD_DIM
    heads = _pair_heads()
    n_hp = N_KV_HEADS // 2
    for hp in range(n_hp):
        cols = slice(hp * LANES, (hp + 1) * LANES)
        kpair = jnp.concatenate([kp_ref[:, cols], kc_ref[:, cols], kn_ref[:, cols]], axis=0)
        vpair = jnp.concatenate([vp_ref[:, cols], vc_ref[:, cols], vn_ref[:, cols]], axis=0)
        zero = jnp.zeros_like(kpair)
        one = jnp.ones_like(vpair)
        k_scr[hp, 0] = jnp.where(low_kv, kpair, zero)
        k_scr[hp, 1] = jnp.where(low_kv, zero, kpair)
        v_scr[hp, 0] = jnp.where(low_kv, vpair, one)
        v_scr[hp, 1] = jnp.where(low_kv, one, vpair)
    tiles = [(hp, g, part) for hp in range(n_hp) for g in range(GQ) for part in range(2)]
    for t, (hp, g, part) in enumerate(tiles):
        pair = hp * GQ + g
        qp = q_ref[:, pair * LANES:(pair + 1) * LANES]
        s = lax.dot_general(qp, k_scr[hp, part], (((1,), (1,)), ((), ())), preferred_element_type=F32)
        s_scr[t] = s + bias_ref[heads[pair][part]]
    for t, (hp, g, part) in enumerate(tiles):
        sink = sink_ref[0, heads[hp * GQ + g][part]]
        m = jnp.maximum(jnp.max(s_scr[t], axis=-1, keepdims=True), sink)
        m_scr[t] = jnp.broadcast_to(m, (BLOCK, LANES))
    for t in range(len(tiles)):
        m = m_scr[t]
        for c in range(3):
            cs = slice(c * BLOCK, (c + 1) * BLOCK)
            p_scr[t, :, cs] = jnp.exp2(s_scr[t, :, cs] - m).astype(BF16)
    for t, (hp, g, part) in enumerate(tiles):
        pair = hp * GQ + g
        sink = sink_ref[0, heads[pair][part]]
        oe = jnp.dot(p_scr[t], v_scr[hp, part], preferred_element_type=F32)
        o = oe / (pltpu.roll(oe, HEAD_DIM, 1) + jnp.exp2(sink - m_scr[t]))
        if part == 0:
            first = o
        else:
            o_ref[:, pair * LANES:(pair + 1) * LANES] = jnp.where(low_q, first, o).astype(o_ref.dtype)


def _attention(q, k, v, sink, bsz, lp):
    nb = lp // BLOCK
    dq = N_Q_HEADS * HEAD_DIM
    dkv = N_KV_HEADS * HEAD_DIM
    q3, k3, v3 = q.reshape(bsz, lp, dq), k.reshape(bsz, lp, dkv), v.reshape(bsz, lp, dkv)
    sink2 = (sink.astype(F32) * LOG2E).reshape(1, N_Q_HEADS)
    kv_prev = pl.BlockSpec((None, BLOCK, dkv), lambda b, n: (b, jnp.maximum(n - 1, 0), 0))
    kv_cur = pl.BlockSpec((None, BLOCK, dkv), lambda b, n: (b, n, 0))
    kv_next = pl.BlockSpec((None, BLOCK, dkv), lambda b, n: (b, jnp.minimum(n + 1, nb - 1), 0))

    def bias_case(b, n):
        return (jnp.where(n == 0, 0, jnp.where(n == 1, 1, jnp.where(n == nb - 1, 3, 2))), 0, 0, 0)
    out = pl.pallas_call(
        _attn_kernel,
        grid=(bsz, nb),
        in_specs=[pl.BlockSpec((None, BLOCK, dq), lambda b, n: (b, n, 0)),
                  kv_prev, kv_cur, kv_next, kv_prev, kv_cur, kv_next,
                  pl.BlockSpec((None, N_Q_HEADS, BLOCK, 3 * BLOCK), bias_case),
                  pl.BlockSpec(memory_space=pltpu.SMEM)],
        out_specs=pl.BlockSpec((None, BLOCK, dq), lambda b, n: (b, n, 0)),
        out_shape=jax.ShapeDtypeStruct((bsz, lp, dq), BF16),
        scratch_shapes=[pltpu.VMEM((N_KV_HEADS // 2, 2, 3 * BLOCK, LANES), BF16),
                        pltpu.VMEM((N_KV_HEADS // 2, 2, 3 * BLOCK, LANES), BF16),
                        pltpu.VMEM((N_Q_HEADS, BLOCK, 3 * BLOCK), F32),
                        pltpu.VMEM((N_Q_HEADS, BLOCK, LANES), F32),
                        pltpu.VMEM((N_Q_HEADS, BLOCK, 3 * BLOCK), BF16)],
        compiler_params=_cparams(2),
        name="swa",
    )(q3, k3, k3, k3, v3, v3, v3, _attn_bias(lp), sink2)
    return out.reshape(bsz * lp, dq)


def _ffn_kernel(*refs, lp, tm, n_tiles, final, n_cur):
    hp_ref, hc_refs, hn_ref = refs[0], refs[1:1 + n_cur], refs[1 + n_cur]
    refs = refs[2 + n_cur:]
    i = pl.program_id(1 if final else 0)
    n = tm + 2 * FFN_HALO
    hp, hn = hp_ref[...], hn_ref[...]
    hc = jnp.concatenate([r[...] for r in hc_refs], axis=0) if n_cur > 1 else hc_refs[0][...]
    if final:
        op_ref, oc_refs, on_ref, wo_ref = refs[0], refs[1:1 + n_cur], refs[1 + n_cur], refs[2 + n_cur]
        refs, oext = refs[3 + n_cur:-1], refs[-1]
        oext[0:FFN_HALO, :] = op_ref[...]
        for k, r in enumerate(oc_refs):
            oext[FFN_HALO + k * BLOCK:FFN_HALO + (k + 1) * BLOCK, :] = r[...]
        oext[FFN_HALO + tm:n, :] = on_ref[...]
        proj = jnp.dot(oext[...], wo_ref[...], preferred_element_type=F32)
        hp, hc, hn = hp + proj[0:FFN_HALO], hc + proj[FFN_HALO:FFN_HALO + tm], hn + proj[FFN_HALO + tm:n]
    g_ref, wup_ref, cw_ref, cb_ref, wdn_ref, gf_ref, o_ref, nat_scr, hx_scr, act_scr = refs
    g = g_ref[...]
    nv = n // SUBLANES
    nslab = D_MODEL // LANES

    def put(r0, x):
        for l in range(nslab):
            nat_scr[l, r0:r0 + x.shape[0], :] = x[:, l * LANES:(l + 1) * LANES]
    keep = jnp.where(i < n_tiles - 1, 1.0, 0.0)
    put(0, _rmsnorm(hp, g))
    put(FFN_HALO, _rmsnorm(hc, g))
    put(FFN_HALO + tm, _rmsnorm(hn, g) * keep)

    def to_strided(jp, _):
        r = pl.ds(pl.multiple_of(jp * BF16_ROWS, BF16_ROWS), BF16_ROWS)
        for l in range(nslab):
            two = jnp.concatenate([nat_scr[l, pl.ds(2 * jp, SUBLANES, stride=nv), :],
                                   nat_scr[l, pl.ds(2 * jp + 1, SUBLANES, stride=nv), :]], axis=0)
            hx_scr[r, l * LANES:(l + 1) * LANES] = two.astype(BF16)
        return 0
    lax.fori_loop(0, nv // 2, to_strided, 0)

    def conv(u, cols):
        prev = jnp.concatenate([pltpu.roll(u[n - SUBLANES:], 1, 0), u[:n - SUBLANES]], axis=0)
        nxt = jnp.concatenate([u[SUBLANES:], pltpu.roll(u[:SUBLANES], SUBLANES - 1, 0)], axis=0)
        return (cw_ref[0:1, cols] * prev + cw_ref[1:2, cols] * u + cw_ref[2:3, cols] * nxt) + cb_ref[0:1, cols]

    for c in range(D_FF // FFN_TF):
        ca = slice(c * FFN_TF, (c + 1) * FFN_TF)
        cg = slice(D_FF + c * FFN_TF, D_FF + (c + 1) * FFN_TF)
        ua = jnp.dot(hx_scr[...], wup_ref[:, ca], preferred_element_type=F32)
        ug = jnp.dot(hx_scr[...], wup_ref[:, cg], preferred_element_type=F32)
        act_scr[:, ca] = (_gelu(conv(ug, cg)) * conv(ua, ca)).astype(BF16)

    ys = jnp.dot(act_scr[...], wdn_ref[...], preferred_element_type=F32)
    for j in range(nv):
        for l in range(nslab):
            nat_scr[l, pl.ds(j, SUBLANES, stride=nv), :] = ys[j * SUBLANES:(j + 1) * SUBLANES,
                                                              l * LANES:(l + 1) * LANES]
    y = hc + jnp.concatenate([nat_scr[l, FFN_HALO:FFN_HALO + tm, :] for l in range(nslab)], axis=1)
    if final:
        o_ref[...] = _rmsnorm(y, gf_ref[...])
    else:
        o_ref[...] = jnp.where(_valid_rows(i, tm, lp, y.shape), y, 0.0)


def _conv_ffn(h2, g, w_up, conv_w, conv_b, w_down, g_final, bsz, lp, final, o=None, w_o=None):
    rows, d = h2.shape
    const = dict(pipeline_mode=pl.Buffered(1))
    cw = jnp.pad(conv_w.astype(F32), ((0, SUBLANES - CONV_F), (0, 0)))
    if final:
        seq = lp - FRONT
        tm = _row_tile(seq, seq, mult=BLOCK)
        n_tiles, n_cur = seq // tm, tm // BLOCK
        grid = (bsz, n_tiles)
        last = lp // FFN_HALO - 1
        h_in = h2.reshape(bsz, lp, d)
        prev = pl.BlockSpec((None, FFN_HALO, d), lambda b, j: (b, (FRONT + j * tm) // FFN_HALO - 1, 0))
        cur = [pl.BlockSpec((None, BLOCK, d), lambda b, j, k=k: (b, 1 + j * n_cur + k, 0)) for k in range(n_cur)]
        nxt = pl.BlockSpec((None, FFN_HALO, d),
                           lambda b, j: (b, jnp.minimum((FRONT + (j + 1) * tm) // FFN_HALO, last), 0))
        out_spec = pl.BlockSpec((None, tm, d), lambda b, j: (b, j, 0))
        out_shape = jax.ShapeDtypeStruct((bsz, seq, d), F32)

        def whole(shape):
            return pl.BlockSpec(shape, lambda b, j: (0,) * len(shape))
        extra_specs = [prev] + cur + [nxt, pl.BlockSpec(w_o.shape, whole(w_o.shape).index_map, **const)]
        extra_ops = [o.reshape(bsz, lp, d)] * (2 + n_cur) + [w_o.astype(BF16)]
        extra_scratch = [pltpu.VMEM((tm + 2 * FFN_HALO, d), BF16)]
    else:
        tm = _row_tile(rows, lp, cap=FFN_TM)
        n_tiles, n_cur = rows // tm, 1
        grid = (n_tiles,)
        hb = tm // FFN_HALO
        last = rows // FFN_HALO - 1
        h_in = h2
        prev = pl.BlockSpec((FFN_HALO, d), lambda i: (jnp.maximum(i * hb - 1, 0), 0))
        cur = [pl.BlockSpec((tm, d), lambda i: (i, 0))]
        nxt = pl.BlockSpec((FFN_HALO, d), lambda i: (jnp.minimum((i + 1) * hb, last), 0))
        out_spec = pl.BlockSpec((tm, d), lambda i: (i, 0))
        out_shape = jax.ShapeDtypeStruct((rows, d), F32)

        def whole(shape):
            return pl.BlockSpec(shape, lambda i: (0,) * len(shape))
        extra_specs, extra_ops, extra_scratch = [], [], []
    weights = [pl.BlockSpec((1, d), whole((1, d)).index_map),
               pl.BlockSpec((d, 2 * D_FF), whole((d, 2 * D_FF)).index_map, **const),
               whole((SUBLANES, 2 * D_FF)), whole((1, 2 * D_FF)),
               pl.BlockSpec((D_FF, d), whole((D_FF, d)).index_map, **const),
               whole((1, d))]
    return pl.pallas_call(
        functools.partial(_ffn_kernel, lp=lp, tm=tm, n_tiles=n_tiles, final=final, n_cur=n_cur),
        grid=grid,
        in_specs=[prev] + cur + [nxt] + extra_specs + weights,
        out_specs=out_spec,
        out_shape=out_shape,
        scratch_shapes=[pltpu.VMEM((d // LANES, tm + 2 * FFN_HALO, LANES), F32),
                        pltpu.VMEM((tm + 2 * FFN_HALO, d), BF16),
                        pltpu.VMEM((tm + 2 * FFN_HALO, D_FF), BF16)] + extra_scratch,
        compiler_params=_cparams(len(grid)),
        name="conv_ffn",
    )(*([h_in] * (2 + n_cur)), *extra_ops, g.astype(F32).reshape(1, d), w_up.astype(BF16), cw,
      conv_b.astype(F32).reshape(1, 2 * D_FF), w_down.astype(BF16), g_final.astype(F32).reshape(1, d))


def _trunk(x, prm):
    bsz, seq, d = x.shape
    lp = seq + FRONT
    assert seq % BLOCK == 0 and d == D_MODEL

    h, u_a, x_b, g_b = _in_proj(x, prm['meta_tokens'], prm['norm_mix_g'][0], prm['w_in_ab'][0].astype(BF16), lp)
    s5w = _s5_weights(prm['s5_lambda_re'][0], prm['s5_lambda_im'][0], prm['s5_log_dt'][0],
                      prm['s5_b_re'][0], prm['s5_b_im'][0], prm['s5_c_re'][0], prm['s5_c_im'][0], prm['s5_d'][0])
    y_a = _s5_mixer(u_a, s5w, bsz, lp)
    lruw = _lru_weights(prm['lru_conv_w'][0], prm['lru_conv_b'][0], prm['lru_w_r'][0], prm['lru_b_r'][0],
                        prm['lru_w_i'][0], prm['lru_b_i'][0], prm['lru_lambda'][0])
    hf, hb = _lru_mixer(x_b, lruw, bsz, lp)
    h = _ab_out(y_a, hf, hb, g_b, h, prm['w_glu'][0], prm['b_glu'][0], prm['w_out_ab'][0], lp)
    h = _conv_ffn(h, prm['norm_ffn_g'][0], prm['w_up'][0], prm['ffn_conv_w'][0], prm['ffn_conv_b'][0],
                  prm['w_down'][0], prm['final_norm_g'], bsz, lp, final=False)

    dq, dkv = N_Q_HEADS * HEAD_DIM, N_KV_HEADS * HEAD_DIM
    head_order = jnp.asarray([h_ for pair in _pair_heads() for h_ in pair])
    col_order = (head_order[:, None] * HEAD_DIM + jnp.arange(HEAD_DIM)[None, :]).reshape(-1)
    w_qkv = prm['w_qkv'][0]
    w_qkv = jnp.concatenate([w_qkv[:, :dq][:, col_order], w_qkv[:, dq:]], axis=1).astype(BF16)
    q, k, v = _norm_mm(h, prm['norm_mix_g'][1], w_qkv, (dq, dkv, dkv),
                       (LOG2E / math.sqrt(HEAD_DIM), 1.0, 1.0), lp)
    o = _attention(q, k, v, prm['attn_sink'][0], bsz, lp)
    return _conv_ffn(h, prm['norm_ffn_g'][1], prm['w_up'][1], prm['ffn_conv_w'][1], prm['ffn_conv_b'][1],
                     prm['w_down'][1], prm['final_norm_g'], bsz, lp, final=True, o=o, w_o=prm['w_o'][0][col_order])


def kernel(x_prompt, x_sample, meta_tokens, norm_mix_g, norm_ffn_g, final_norm_g, w_in_ab, s5_lambda_re, s5_lambda_im, s5_log_dt, s5_b_re, s5_b_im, s5_c_re, s5_c_im, s5_d, w_glu, b_glu, lru_conv_w, lru_conv_b, lru_w_r, lru_b_r, lru_w_i, lru_b_i, lru_lambda, w_out_ab, w_qkv, w_o, attn_sink, w_up, ffn_conv_w, ffn_conv_b, w_down):
    prm = dict(meta_tokens=meta_tokens, norm_mix_g=norm_mix_g, norm_ffn_g=norm_ffn_g, final_norm_g=final_norm_g,
               w_in_ab=w_in_ab, s5_lambda_re=s5_lambda_re, s5_lambda_im=s5_lambda_im, s5_log_dt=s5_log_dt,
               s5_b_re=s5_b_re, s5_b_im=s5_b_im, s5_c_re=s5_c_re, s5_c_im=s5_c_im, s5_d=s5_d, w_glu=w_glu,
               b_glu=b_glu, lru_conv_w=lru_conv_w, lru_conv_b=lru_conv_b, lru_w_r=lru_w_r, lru_b_r=lru_b_r,
               lru_w_i=lru_w_i, lru_b_i=lru_b_i, lru_lambda=lru_lambda, w_out_ab=w_out_ab, w_qkv=w_qkv,
               w_o=w_o, attn_sink=attn_sink, w_up=w_up, ffn_conv_w=ffn_conv_w, ffn_conv_b=ffn_conv_b,
               w_down=w_down)
    return (_trunk(x_prompt, prm), _trunk(x_sample, prm))
```

```python
import functools
import math

import jax
import jax.numpy as jnp
from jax import lax
from jax.experimental import pallas as pl
from jax.experimental.pallas import tpu as pltpu

D_MODEL = 1024
N_META = 16
D_A = 512
S5_GROUP = 16
G_A = D_A // S5_GROUP
S5_STATE = 64
D_B = 512
H_B = 8
BW_B = D_B // H_B
LRU_C = 8.0
CONV_B = 4
CONV_B_LEFT = 2
HEAD_DIM = 64
N_Q_HEADS = 16
N_KV_HEADS = 4
GQ = N_Q_HEADS // N_KV_HEADS
WINDOW = 128
BLOCK = 128
D_FF = 2816
CONV_F = 3
EPS = 1e-6
NEG = -1e30

PAD = BLOCK - N_META
FRONT = PAD + N_META
S5_T = 16
S5_W = S5_T * S5_GROUP
LANES = 128
SUBLANES = 8
BF16_ROWS = 16
FFN_HALO = BF16_ROWS
FFN_TF = 256
FFN_TM = 1024
LRU_CB = LANES
VMEM_LIMIT = 56 * 1024 * 1024

F32 = jnp.float32
BF16 = jnp.bfloat16


def _cparams(n_axes):
    return pltpu.CompilerParams(
        dimension_semantics=("arbitrary",) * n_axes, vmem_limit_bytes=VMEM_LIMIT)


def _row_tile(rows, lp, cap=512, mult=BF16_ROWS):
    best = None
    t = mult
    while t <= min(cap, lp):
        if rows % t == 0:
            best = t
        t += mult
    assert best is not None, (rows, lp, cap, mult)
    return best


def _lru_tile(lp):
    q = lp // 32
    assert lp % 32 == 0
    odd = q
    while odd % 2 == 0:
        odd //= 2
    best = 1
    for d in range(1, odd + 1, 2):
        if odd % d == 0 and 32 * d <= 640:
            best = d
    return 32 * best


def _rmsnorm(x, g):
    ms = jnp.mean(x * x, axis=-1, keepdims=True)
    return x * lax.rsqrt(ms + EPS) * g


def _gelu(x):
    k1 = -2.0 * math.sqrt(2.0 / math.pi) * math.log2(math.e)
    return x / (1.0 + jnp.exp2(x * (k1 + (k1 * 0.044715) * (x * x))))


def _sigmoid(x):
    return 1.0 / (1.0 + jnp.exp(-x))


def _valid_rows(tile_idx, tm, lp, shape):
    pos = lax.rem(tile_idx * tm, lp) + lax.broadcasted_iota(jnp.int32, shape, 0)
    pos = jnp.where(pos >= lp, pos - lp, pos)
    return pos >= PAD


def _norm_mm_kernel(h_ref, g_ref, w_ref, *o_refs, scales):
    hn = _rmsnorm(h_ref[...], g_ref[...]).astype(BF16)
    off = 0
    for o_ref, sc in zip(o_refs, scales):
        n = o_ref.shape[-1]
        z = jnp.dot(hn, w_ref[:, off:off + n], preferred_element_type=F32)
        if sc != 1.0:
            z = z * sc
        o_ref[...] = z.astype(o_ref.dtype)
        off += n


def _norm_mm(h2, g, w, splits, scales, lp):
    rows, d = h2.shape
    tm = _row_tile(rows, lp, cap=FFN_TM)
    n = w.shape[1]
    return pl.pallas_call(
        functools.partial(_norm_mm_kernel, scales=scales),
        grid=(rows // tm,),
        in_specs=[pl.BlockSpec((tm, d), lambda i: (i, 0)),
                  pl.BlockSpec((1, d), lambda i: (0, 0)),
                  pl.BlockSpec((d, n), lambda i: (0, 0))],
        out_specs=[pl.BlockSpec((tm, s), lambda i: (i, 0)) for s in splits],
        out_shape=[jax.ShapeDtypeStruct((rows, s), BF16) for s in splits],
        compiler_params=_cparams(1),
        name="norm_mm",
    )(h2, g.reshape(1, d), w)


UNIT = S5_T * BF16_ROWS
GRP_PER_VREG = LANES // S5_GROUP


def _unit_perm():
    r = jnp.arange(UNIT)
    src = (r % BF16_ROWS) * S5_T + r // BF16_ROWS
    return (src[:, None] == jnp.arange(UNIT)[None, :]).astype(BF16)


def _tokens_to_chunks(zp, store):
    grp = lax.broadcasted_iota(jnp.int32, (BF16_ROWS, LANES), 1) // S5_GROUP
    for j in range(D_A // LANES):
        v = [zp[s * BF16_ROWS:(s + 1) * BF16_ROWS, j * LANES:(j + 1) * LANES] for s in range(S5_T)]
        for gl in range(GRP_PER_VREG):
            for half in range(S5_T // GRP_PER_VREG):
                acc = None
                for p in range(GRP_PER_VREG):
                    src = v[half * GRP_PER_VREG + p]
                    shift = S5_GROUP * ((p - gl) % GRP_PER_VREG)
                    val = src if shift == 0 else pltpu.roll(src, shift, 1)
                    acc = val if acc is None else jnp.where(grp == p, val, acc)
                store(j * GRP_PER_VREG + gl, half, acc)


def _chunks_to_tokens(load):
    grp = lax.broadcasted_iota(jnp.int32, (BF16_ROWS, LANES), 1) // S5_GROUP
    cols = []
    for j in range(D_A // LANES):
        rows = []
        for half in range(S5_T // GRP_PER_VREG):
            y = [load(j * GRP_PER_VREG + gl, half) for gl in range(GRP_PER_VREG)]
            for p in range(GRP_PER_VREG):
                acc = None
                for gl in range(GRP_PER_VREG):
                    shift = S5_GROUP * ((gl - p) % GRP_PER_VREG)
                    val = y[gl] if shift == 0 else pltpu.roll(y[gl], shift, 1)
                    acc = val if acc is None else jnp.where(grp == gl, val, acc)
                rows.append(acc)
        cols.append(jnp.concatenate(rows, axis=0))
    return jnp.concatenate(cols, axis=1)


def _in_proj_kernel(*refs, tm, nb):
    n_cur = tm // BLOCK
    x_refs = refs[:n_cur]
    meta_ref, g_ref, w_ref, perm_ref, h_ref, u_ref, xb_ref, gb_ref = refs[n_cur:]
    first = pl.program_id(0) * n_cur
    blocks = [jnp.where(lax.rem(first + k, nb) == 0, meta_ref[...], x_refs[k][...]) for k in range(n_cur)]
    h = jnp.concatenate(blocks, axis=0)
    h_ref[...] = h
    hn = _rmsnorm(h, g_ref[...]).astype(BF16)
    xb_ref[...] = jnp.dot(hn, w_ref[:, D_A:D_A + D_B], preferred_element_type=F32).astype(BF16)
    gb_ref[...] = jnp.dot(hn, w_ref[:, D_A + D_B:], preferred_element_type=F32).astype(BF16)
    za = jnp.dot(hn, w_ref[:, 0:D_A], preferred_element_type=F32).astype(BF16)
    for q in range(tm // UNIT):
        zp = jnp.dot(perm_ref[...], za[q * UNIT:(q + 1) * UNIT], preferred_element_type=F32)

        def store(g, half, x, q=q):
            u_ref[g, q * BF16_ROWS:(q + 1) * BF16_ROWS, half * LANES:(half + 1) * LANES] = x.astype(BF16)
        _tokens_to_chunks(zp, store)


def _in_proj(x, meta, g, w, lp):
    bsz, seq, d = x.shape
    rows = bsz * lp
    nb = lp // BLOCK
    tm = _row_tile(rows, lp, mult=UNIT)
    n_cur = tm // BLOCK
    n = w.shape[1]
    meta_blk = jnp.pad(meta.astype(F32), ((PAD, 0), (0, 0)))
    x_specs = [pl.BlockSpec((None, BLOCK, d),
                            lambda i, k=k: ((i * n_cur + k) // nb, jnp.maximum((i * n_cur + k) % nb - 1, 0), 0))
               for k in range(n_cur)]
    return pl.pallas_call(
        functools.partial(_in_proj_kernel, tm=tm, nb=nb),
        grid=(rows // tm,),
        in_specs=x_specs + [pl.BlockSpec((BLOCK, d), lambda i: (0, 0)),
                            pl.BlockSpec((1, d), lambda i: (0, 0)),
                            pl.BlockSpec((d, n), lambda i: (0, 0)),
                            pl.BlockSpec((UNIT, UNIT), lambda i: (0, 0))],
        out_specs=[pl.BlockSpec((tm, d), lambda i: (i, 0)),
                   pl.BlockSpec((G_A, tm // S5_T, S5_W), lambda i: (0, i, 0)),
                   pl.BlockSpec((tm, D_B), lambda i: (i, 0)),
                   pl.BlockSpec((tm, D_B), lambda i: (i, 0))],
        out_shape=[jax.ShapeDtypeStruct((rows, d), F32),
                   jax.ShapeDtypeStruct((G_A, rows // S5_T, S5_W), BF16),
                   jax.ShapeDtypeStruct((rows, D_B), BF16),
                   jax.ShapeDtypeStruct((rows, D_B), BF16)],
        compiler_params=_cparams(1),
        name="in_proj",
    )(*([x.astype(F32)] * n_cur), meta_blk, g.reshape(1, d), w, _unit_perm())


def _cmul(ar, ai, br, bi):
    return ar * br - ai * bi, ar * bi + ai * br


def _s5_weights(lam_re, lam_im, log_dt, b_re, b_im, c_re, c_im, d_skip):
    hp = lax.Precision.HIGHEST
    t = S5_T
    lr = jnp.minimum(lam_re.astype(F32), -1e-4)
    li = lam_im.astype(F32)
    dt = jnp.exp(log_dt.astype(F32))[..., None]
    mag = jnp.exp(lr * dt)
    lbr, lbi = mag * jnp.cos(li * dt), mag * jnp.sin(li * dt)
    den = lr * lr + li * li
    xr, xi = lbr - 1.0, lbi
    fr, fi = (xr * lr + xi * li) / den, (xi * lr - xr * li) / den
    bbr, bbi = _cmul(fr[..., None], fi[..., None], b_re.astype(F32), b_im.astype(F32))
    pr, pi = [jnp.ones_like(lbr)], [jnp.zeros_like(lbi)]
    for _ in range(t):
        nr, ni = _cmul(pr[-1], pi[-1], lbr, lbi)
        pr.append(nr)
        pi.append(ni)
    pwr, pwi = jnp.stack(pr, -1), jnp.stack(pi, -1)
    cr, ci = c_re.astype(F32), c_im.astype(F32)
    wr, wi = _cmul(cr[..., None], ci[..., None], pwr[:, :, None], pwi[:, :, None])
    kk = jnp.einsum('dgonk,dgni->dgkio', jnp.concatenate([wr[..., :t], -wi[..., :t]], axis=3),
                    jnp.concatenate([bbr, bbi], axis=2), precision=hp)
    eye = jnp.eye(S5_GROUP, dtype=F32) * d_skip.astype(F32).reshape(G_A, 1, S5_GROUP)
    k_all = jnp.concatenate([kk[1][:, 1:][:, ::-1],
                             (kk[0][:, 0] + kk[1][:, 0] + eye)[:, None],
                             kk[0][:, 1:]], axis=1)
    sidx = jnp.arange(t)
    lag = sidx[None, :] - sidx[:, None] + (t - 1)
    m = k_all[:, lag]
    m = m.transpose(0, 1, 3, 2, 4).reshape(G_A, S5_W, S5_W)
    pfr, pfi = _cmul(pwr[0][..., :t][..., ::-1][..., None], pwi[0][..., :t][..., ::-1][..., None],
                     bbr[0][:, :, None, :], bbi[0][:, :, None, :])
    pbr, pbi = _cmul(pwr[1][..., :t][..., None], pwi[1][..., :t][..., None],
                     bbr[1][:, :, None, :], bbi[1][:, :, None, :])

    def p_cols(x):
        x = x.transpose(0, 2, 3, 1).reshape(G_A, S5_W, S5_STATE)
        return jnp.pad(x, ((0, 0), (0, 0), (0, LANES - S5_STATE)))
    p = jnp.concatenate([p_cols(pfr), p_cols(pfi), p_cols(pbr), p_cols(pbi)], axis=-1)

    def q_rows(x):
        x = x.transpose(0, 2, 3, 1).reshape(G_A, S5_STATE, S5_W)
        return jnp.pad(x, ((0, 0), (0, LANES - S5_STATE), (0, 0)))
    q = jnp.concatenate([q_rows(wr[0][..., 1:]), q_rows(-wi[0][..., 1:]),
                         q_rows(wr[1][..., 1:][..., ::-1]), q_rows(-wi[1][..., 1:][..., ::-1])], axis=1)

    def c_row(x):
        return jnp.pad(x, ((0, 0), (0, LANES - S5_STATE)))[:, None]
    coef = jnp.concatenate([c_row(pwr[0][..., t]), c_row(pwi[0][..., t]),
                            c_row(pwr[1][..., t]), c_row(pwi[1][..., t]),
                            jnp.zeros((G_A, 4, LANES), F32)], axis=1)
    return m.astype(BF16), p.astype(BF16), q.astype(BF16), coef


def _s5_kernel(u_ref, m_ref, p_ref, q_ref, coef_ref, y_ref, s_scr, x0_scr, *, n_chunks, bsz, rb):
    rows = n_chunks * bsz
    ncomp = 4
    for k in range(rows // rb):
        sl = slice(k * rb, (k + 1) * rb)
        s = jnp.dot(u_ref[sl, :], p_ref[...], preferred_element_type=F32)
        for c in range(ncomp):
            s_scr[c, sl, :] = s[:, c * LANES:(c + 1) * LANES]

    shp = (bsz, LANES)
    lfr = jnp.broadcast_to(coef_ref[0:1, :], shp)
    lfi = jnp.broadcast_to(coef_ref[1:2, :], shp)
    lbr = jnp.broadcast_to(coef_ref[2:3, :], shp)
    lbi = jnp.broadcast_to(coef_ref[3:4, :], shp)

    def step(j, carry):
        xfr, xfi, xbr, xbi = carry
        rf = pl.ds(j, bsz, stride=n_chunks)
        rbk = pl.ds(n_chunks - 1 - j, bsz, stride=n_chunks)
        x0_scr[0, rf, :] = xfr
        x0_scr[1, rf, :] = xfi
        x0_scr[2, rbk, :] = xbr
        x0_scr[3, rbk, :] = xbi
        nfr = lfr * xfr - lfi * xfi + s_scr[0, rf, :]
        nfi = lfr * xfi + lfi * xfr + s_scr[1, rf, :]
        nbr = lbr * xbr - lbi * xbi + s_scr[2, rbk, :]
        nbi = lbr * xbi + lbi * xbr + s_scr[3, rbk, :]
        return nfr, nfi, nbr, nbi

    z = jnp.zeros(shp, F32)
    lax.fori_loop(0, n_chunks, step, (z, z, z, z), unroll=4 if n_chunks % 4 == 0 else 1)

    for k in range(rows // rb):
        sl = slice(k * rb, (k + 1) * rb)
        x0 = jnp.concatenate([x0_scr[c, sl, :] for c in range(ncomp)], axis=1).astype(BF16)
        y = jnp.dot(u_ref[sl, :], m_ref[...], preferred_element_type=F32)
        y = y + jnp.dot(x0, q_ref[...], preferred_element_type=F32)
        y_ref[sl, :] = y.astype(y_ref.dtype)


def _s5_mixer(ut, weights, bsz, lp):
    m, p, q, coef = weights
    n_chunks = lp // S5_T
    rows = n_chunks * bsz
    rb = _row_tile(rows, rows, cap=640)
    return pl.pallas_call(
        functools.partial(_s5_kernel, n_chunks=n_chunks, bsz=bsz, rb=rb),
        grid=(G_A,),
        in_specs=[pl.BlockSpec((None, rows, S5_W), lambda g: (g, 0, 0)),
                  pl.BlockSpec((None, S5_W, S5_W), lambda g: (g, 0, 0)),
                  pl.BlockSpec((None, S5_W, 4 * LANES), lambda g: (g, 0, 0)),
                  pl.BlockSpec((None, 4 * LANES, S5_W), lambda g: (g, 0, 0)),
                  pl.BlockSpec((None, SUBLANES, LANES), lambda g: (g, 0, 0))],
        out_specs=pl.BlockSpec((None, rows, S5_W), lambda g: (g, 0, 0)),
        out_shape=jax.ShapeDtypeStruct((G_A, rows, S5_W), BF16),
        scratch_shapes=[pltpu.VMEM((4, rows, LANES), F32), pltpu.VMEM((4, rows, LANES), F32)],
        compiler_params=_cparams(1),
        name="s5_chunks",
    )(ut, m, p, q, coef)


def _lru_weights(conv_w, conv_b, w_r, b_r, w_i, b_i, lam):
    ncb = D_B // LRU_CB
    hpb = LRU_CB // BW_B

    def blockdiag(w):
        w = w.astype(F32).reshape(2, ncb, hpb, BW_B, BW_B)
        out = jnp.zeros((2, ncb, hpb, BW_B, hpb, BW_B), F32)
        for k in range(hpb):
            out = out.at[:, :, k, :, k, :].set(w[:, :, k])
        return out.reshape(2, ncb, LRU_CB, LRU_CB)
    wg = jnp.concatenate([blockdiag(w_r), blockdiag(w_i)], axis=-1).astype(BF16)
    bg = jnp.concatenate([b_r.astype(F32).reshape(2, ncb, 1, LRU_CB),
                          b_i.astype(F32).reshape(2, ncb, 1, LRU_CB)], axis=-1)
    cl = (-LRU_C * LOG2E * jax.nn.softplus(-lam.astype(F32))).reshape(2, ncb, 1, LRU_CB)
    cw = jnp.pad(conv_w.astype(F32), ((0, SUBLANES - CONV_B), (0, 0)))
    cb = conv_b.astype(F32).reshape(1, D_B)
    return cw, cb, wg, bg, cl


LRU_HALO = BF16_ROWS
N_CB = D_B // LRU_CB


def _lru_kernel(fp_ref, fc_ref, fn_ref, bp_ref, bc_ref, bn_ref, cw_ref, cb_ref, wg_ref, bg_ref, cl_ref,
                hf_ref, hb_ref, xw, a_scr, b_scr, h_scr, init_scr, carry_scr, *, tt, n_tiles):
    t = pl.program_id(1)
    seg = tt // SUBLANES
    sb = tt // 4
    halo = LRU_HALO
    dirs = ((fp_ref, fc_ref, fn_ref, t), (bp_ref, bc_ref, bn_ref, n_tiles - 1 - t))

    @pl.when(t == 0)
    def _():
        carry_scr[...] = jnp.zeros_like(carry_scr)

    for d, (p_ref, c_ref, n_ref, ti) in enumerate(dirs):
        keep = jnp.where(ti < n_tiles - 1, 1.0, 0.0)
        xw[d, 0:halo, :] = p_ref[...].astype(F32)
        xw[d, halo:halo + tt, :] = c_ref[...].astype(F32)
        xw[d, halo + tt:2 * halo + tt, :] = n_ref[...].astype(F32) * keep

    def gates(k, _):
        n = sb + 2 * SUBLANES
        base = pl.multiple_of(k * sb, SUBLANES)
        rows = pl.ds(base, sb)
        for d in range(2):
            for cb in range(N_CB):
                lanes = slice(cb * LRU_CB, (cb + 1) * LRU_CB)
                win = xw[d, pl.ds(base + halo - SUBLANES, n), lanes]
                lo, hi = SUBLANES, SUBLANES + sb
                xc = (cw_ref[0:1, lanes] * pltpu.roll(win, 2, 0)[lo:hi]
                      + cw_ref[1:2, lanes] * pltpu.roll(win, 1, 0)[lo:hi]
                      + cw_ref[2:3, lanes] * win[lo:hi]
                      + cw_ref[3:4, lanes] * pltpu.roll(win, n - 1, 0)[lo:hi]) + cb_ref[0:1, lanes]
                z = jnp.dot(xc.astype(BF16), wg_ref[d, cb], preferred_element_type=F32) + bg_ref[d, cb]
                r = _sigmoid(z[:, :LRU_CB])
                ig = _sigmoid(z[:, LRU_CB:])
                a = jnp.exp2(cl_ref[d, cb] * r)
                y = 1.0 - a * a
                root = jnp.where(y > 0.0, y * lax.rsqrt(y), 0.0)
                a_scr[d, cb, rows, :] = a
                b_scr[d, cb, rows, :] = root * (ig * xc)
        return 0
    lax.fori_loop(0, tt // sb, gates, 0)

    for k in range(-(-PAD // tt)):
        npad = min(tt, PAD - k * tt)

        @pl.when(t == k)
        def _(npad=npad):
            for cb in range(N_CB):
                b_scr[0, cb, 0:npad, :] = jnp.zeros((npad, LANES), F32)

    chains = [(d, cb) for d in range(2) for cb in range(N_CB)]

    def rows_of(d, i):
        return pl.ds(i if d == 0 else seg - 1 - i, SUBLANES, stride=seg)

    def p1(i, c):
        out = []
        for n, (d, cb) in enumerate(chains):
            h, acc = c[2 * n], c[2 * n + 1]
            a = a_scr[d, cb, rows_of(d, i), :]
            out += [a * h + b_scr[d, cb, rows_of(d, i), :], a * acc]
        return tuple(out)

    z8 = jnp.zeros((SUBLANES, LANES), F32)
    o8 = jnp.ones((SUBLANES, LANES), F32)
    ends = lax.fori_loop(0, seg, p1, (z8, o8) * len(chains))
    for n, (d, cb) in enumerate(chains):
        he, ae = ends[2 * n], ends[2 * n + 1]
        c = carry_scr[d, cb, 0:1, :]
        for s in (range(SUBLANES) if d == 0 else reversed(range(SUBLANES))):
            init_scr[d, cb, s:s + 1, :] = c
            c = he[s:s + 1] + ae[s:s + 1] * c
        carry_scr[d, cb, 0:1, :] = c

    def p2(i, c):
        out = []
        for n, (d, cb) in enumerate(chains):
            h = a_scr[d, cb, rows_of(d, i), :] * c[n] + b_scr[d, cb, rows_of(d, i), :]
            h_scr[d, cb, rows_of(d, i), :] = h
            out.append(h)
        return tuple(out)
    lax.fori_loop(0, seg, p2, tuple(init_scr[d, cb] for d, cb in chains))

    for d, o_ref in enumerate((hf_ref, hb_ref)):
        for cb in range(N_CB):
            o_ref[:, cb * LRU_CB:(cb + 1) * LRU_CB] = h_scr[d, cb].astype(o_ref.dtype)


def _lru_mixer(x_b, weights, bsz, lp):
    cw, cb, wg, bg, cl = weights
    tt = _lru_tile(lp)
    n_tiles = lp // tt
    hpt = tt // LRU_HALO
    last = lp // LRU_HALO - 1
    x3 = x_b.reshape(bsz, lp, D_B)

    def specs(tile_of):
        return [pl.BlockSpec((None, LRU_HALO, D_B), lambda b, t: (b, jnp.maximum(tile_of(t) * hpt - 1, 0), 0)),
                pl.BlockSpec((None, tt, D_B), lambda b, t: (b, tile_of(t), 0)),
                pl.BlockSpec((None, LRU_HALO, D_B), lambda b, t: (b, jnp.minimum((tile_of(t) + 1) * hpt, last), 0))]
    fwd = specs(lambda t: t)
    bwd = specs(lambda t: n_tiles - 1 - t)

    def whole(a):
        return pl.BlockSpec(a.shape, lambda b, t: (0,) * a.ndim)
    hf, hb = pl.pallas_call(
        functools.partial(_lru_kernel, tt=tt, n_tiles=n_tiles),
        grid=(bsz, n_tiles),
        in_specs=fwd + bwd + [whole(cw), whole(cb), whole(wg), whole(bg), whole(cl)],
        out_specs=[fwd[1], bwd[1]],
        out_shape=[jax.ShapeDtypeStruct((bsz, lp, D_B), BF16)] * 2,
        scratch_shapes=[pltpu.VMEM((2, tt + 2 * LRU_HALO, D_B), F32),
                        pltpu.VMEM((2, N_CB, tt, LANES), F32), pltpu.VMEM((2, N_CB, tt, LANES), F32),
                        pltpu.VMEM((2, N_CB, tt, LANES), F32),
                        pltpu.VMEM((2, N_CB, SUBLANES, LANES), F32),
                        pltpu.VMEM((2, N_CB, SUBLANES, LANES), F32)],
        compiler_params=_cparams(2),
        name="rglru",
    )(x3, x3, x3, x3, x3, x3, cw, cb, wg, bg, cl)
    return hf.reshape(bsz * lp, D_B), hb.reshape(bsz * lp, D_B)


def _ab_out_kernel(yt_ref, hf_ref, hb_ref, gb_ref, h_ref, permt_ref, wglu_ref, bglu_ref, wo_ref, o_ref, *, lp, tm):
    yb = ((hf_ref[...].astype(F32) + hb_ref[...].astype(F32)) * _gelu(gb_ref[...].astype(F32))).astype(BF16)
    units = []
    for q in range(tm // UNIT):
        def load(g, half, q=q):
            return yt_ref[g, q * BF16_ROWS:(q + 1) * BF16_ROWS, half * LANES:(half + 1) * LANES].astype(F32)
        ysc = _chunks_to_tokens(load).astype(BF16)
        units.append(jnp.dot(permt_ref[...], ysc, preferred_element_type=F32))
    ya = _gelu(jnp.concatenate(units, axis=0))
    gate = _sigmoid(jnp.dot(ya.astype(BF16), wglu_ref[...], preferred_element_type=F32) + bglu_ref[...])
    ya = (ya * gate).astype(BF16)
    acc = jnp.dot(ya, wo_ref[0:D_A, :], preferred_element_type=F32)
    acc = acc + jnp.dot(yb, wo_ref[D_A:, :], preferred_element_type=F32)
    valid = _valid_rows(pl.program_id(0), tm, lp, acc.shape)
    o_ref[...] = jnp.where(valid, h_ref[...] + acc, 0.0)


def _ab_out(yt, hf, hb, g_b, h2, w_glu, b_glu, w_out, lp):
    rows, d = h2.shape
    tm = _row_tile(rows, lp, mult=UNIT)
    return pl.pallas_call(
        functools.partial(_ab_out_kernel, lp=lp, tm=tm),
        grid=(rows // tm,),
        in_specs=[pl.BlockSpec((G_A, tm // S5_T, S5_W), lambda i: (0, i, 0)),
                  pl.BlockSpec((tm, D_B), lambda i: (i, 0)),
                  pl.BlockSpec((tm, D_B), lambda i: (i, 0)),
                  pl.BlockSpec((tm, D_B), lambda i: (i, 0)),
                  pl.BlockSpec((tm, d), lambda i: (i, 0)),
                  pl.BlockSpec((UNIT, UNIT), lambda i: (0, 0)),
                  pl.BlockSpec((D_A, D_A), lambda i: (0, 0)),
                  pl.BlockSpec((1, D_A), lambda i: (0, 0)),
                  pl.BlockSpec((D_A + D_B, d), lambda i: (0, 0))],
        out_specs=pl.BlockSpec((tm, d), lambda i: (i, 0)),
        out_shape=jax.ShapeDtypeStruct((rows, d), F32),
        compiler_params=_cparams(1),
        name="ab_out",
    )(yt, hf, hb, g_b, h2, _unit_perm().T, w_glu.astype(BF16), b_glu.astype(F32).reshape(1, D_A),
      w_out.astype(BF16))


LOG2E = math.log2(math.e)
N_PAIRS = N_Q_HEADS // 2


def _pair_heads():
    return [((2 * hp) * GQ + g, (2 * hp + 1) * GQ + g) for hp in range(N_KV_HEADS // 2) for g in range(GQ)]


def _attn_bias(lp):
    nb = lp // BLOCK
    assert nb >= 3
    qi = jnp.arange(BLOCK)[:, None]
    ki = jnp.arange(3 * BLOCK)[None, :]
    dist = jnp.abs(qi + BLOCK - ki)
    slopes = 2.0 ** (-8.0 * jnp.arange(1, N_Q_HEADS + 1, dtype=F32) / N_Q_HEADS)
    tables = []
    for n in (0, 1, 2 if nb > 3 else None, nb - 1):
        if n is None:
            tables.append(tables[-1])
            continue
        key_pos = (n - 1) * BLOCK + ki
        ok = (dist <= WINDOW) & (key_pos >= PAD) & (key_pos < lp)
        alibi = -slopes[:, None, None] * dist.astype(F32)[None] * LOG2E
        tables.append(jnp.where(ok[None], alibi, NEG))
    return jnp.stack(tables)


def _attn_kernel(q_ref, kp_ref, kc_ref, kn_ref, vp_ref, vc_ref, vn_ref, bias_ref, sink_ref, o_ref,
                 k_scr, v_scr, s_scr, m_scr, p_scr):
    low_kv = lax.broadcasted_iota(jnp.int32, (3 * BLOCK, LANES), 1) < HEAD_DIM
    low_q = lax.broadcasted_iota(jnp.int32, (BLOCK, LANES), 1) < HEAD_DIM
    heads = _pair_heads()
    n_hp = N_KV_HEADS // 2
    for hp in range(n_hp):
        cols = slice(hp * LANES, (hp + 1) * LANES)
        kpair = jnp.concatenate([kp_ref[:, cols], kc_ref[:, cols], kn_ref[:, cols]], axis=0)
        vpair = jnp.concatenate([vp_ref[:, cols], vc_ref[:, cols], vn_ref[:, cols]], axis=0)
        zero = jnp.zeros_like(kpair)
        one = jnp.ones_like(vpair)
        k_scr[hp, 0] = jnp.where(low_kv, kpair, zero)
        k_scr[hp, 1] = jnp.where(low_kv, zero, kpair)
        v_scr[hp, 0] = jnp.where(low_kv, vpair, one)
        v_scr[hp, 1] = jnp.where(low_kv, one, vpair)
    tiles = [(hp, g, part) for hp in range(n_hp) for g in range(GQ) for part in range(2)]
    for t, (hp, g, part) in enumerate(tiles):
        pair = hp * GQ + g
        qp = q_ref[:, pair * LANES:(pair + 1) * LANES]
        s = lax.dot_general(qp, k_scr[hp, part], (((1,), (1,)), ((), ())), preferred_element_type=F32)
        s_scr[t] = s + bias_ref[heads[pair][part]]
    for t, (hp, g, part) in enumerate(tiles):
        sink = sink_ref[0, heads[hp * GQ + g][part]]
        m = jnp.maximum(jnp.max(s_scr[t], axis=-1, keepdims=True), sink)
        m_scr[t] = jnp.broadcast_to(m, (BLOCK, LANES))
    for t in range(len(tiles)):
        m = m_scr[t]
        for c in range(3):
            cs = slice(c * BLOCK, (c + 1) * BLOCK)
            p_scr[t, :, cs] = jnp.exp2(s_scr[t, :, cs] - m).astype(BF16)
    for t, (hp, g, part) in enumerate(tiles):
        pair = hp * GQ + g
        sink = sink_ref[0, heads[pair][part]]
        oe = jnp.dot(p_scr[t], v_scr[hp, part], preferred_element_type=F32)
        o = oe / (pltpu.roll(oe, HEAD_DIM, 1) + jnp.exp2(sink - m_scr[t]))
        if part == 0:
            first = o
        else:
            o_ref[:, pair * LANES:(pair + 1) * LANES] = jnp.where(low_q, first, o).astype(o_ref.dtype)


def _attention(q, k, v, sink, bsz, lp):
    nb = lp // BLOCK
    dq = N_Q_HEADS * HEAD_DIM
    dkv = N_KV_HEADS * HEAD_DIM
    q3, k3, v3 = q.reshape(bsz, lp, dq), k.reshape(bsz, lp, dkv), v.reshape(bsz, lp, dkv)
    sink2 = (sink.astype(F32) * LOG2E).reshape(1, N_Q_HEADS)
    kv_prev = pl.BlockSpec((None, BLOCK, dkv), lambda b, n: (b, jnp.maximum(n - 1, 0), 0))
    kv_cur = pl.BlockSpec((None, BLOCK, dkv), lambda b, n: (b, n, 0))
    kv_next = pl.BlockSpec((None, BLOCK, dkv), lambda b, n: (b, jnp.minimum(n + 1, nb - 1), 0))

    def bias_case(b, n):
        return (jnp.where(n == 0, 0, jnp.where(n == 1, 1, jnp.where(n == nb - 1, 3, 2))), 0, 0, 0)
    out = pl.pallas_call(
        _attn_kernel,
        grid=(bsz, nb),
        in_specs=[pl.BlockSpec((None, BLOCK, dq), lambda b, n: (b, n, 0)),
                  kv_prev, kv_cur, kv_next, kv_prev, kv_cur, kv_next,
                  pl.BlockSpec((None, N_Q_HEADS, BLOCK, 3 * BLOCK), bias_case),
                  pl.BlockSpec(memory_space=pltpu.SMEM)],
        out_specs=pl.BlockSpec((None, BLOCK, dq), lambda b, n: (b, n, 0)),
        out_shape=jax.ShapeDtypeStruct((bsz, lp, dq), BF16),
        scratch_shapes=[pltpu.VMEM((N_KV_HEADS // 2, 2, 3 * BLOCK, LANES), BF16),
                        pltpu.VMEM((N_KV_HEADS // 2, 2, 3 * BLOCK, LANES), BF16),
                        pltpu.VMEM((N_Q_HEADS, BLOCK, 3 * BLOCK), F32),
                        pltpu.VMEM((N_Q_HEADS, BLOCK, LANES), F32),
                        pltpu.VMEM((N_Q_HEADS, BLOCK, 3 * BLOCK), BF16)],
        compiler_params=_cparams(2),
        name="swa",
    )(q3, k3, k3, k3, v3, v3, v3, _attn_bias(lp), sink2)
    return out.reshape(bsz * lp, dq)


def _ffn_kernel(*refs, lp, tm, n_tiles, final, n_cur):
    hp_ref, hc_refs, hn_ref = refs[0], refs[1:1 + n_cur], refs[1 + n_cur]
    refs = refs[2 + n_cur:]
    i = pl.program_id(1 if final else 0)
    n = tm + 2 * FFN_HALO
    hp, hn = hp_ref[...], hn_ref[...]
    hc = jnp.concatenate([r[...] for r in hc_refs], axis=0) if n_cur > 1 else hc_refs[0][...]
    if final:
        op_ref, oc_refs, on_ref, wo_ref = refs[0], refs[1:1 + n_cur], refs[1 + n_cur], refs[2 + n_cur]
        refs, oext = refs[3 + n_cur:-1], refs[-1]
        oext[0:FFN_HALO, :] = op_ref[...]
        for k, r in enumerate(oc_refs):
            oext[FFN_HALO + k * BLOCK:FFN_HALO + (k + 1) * BLOCK, :] = r[...]
        oext[FFN_HALO + tm:n, :] = on_ref[...]
        proj = jnp.dot(oext[...], wo_ref[...], preferred_element_type=F32)
        hp, hc, hn = hp + proj[0:FFN_HALO], hc + proj[FFN_HALO:FFN_HALO + tm], hn + proj[FFN_HALO + tm:n]
    g_ref, wup_ref, cw_ref, cb_ref, wdn_ref, gf_ref, o_ref, nat_scr, hx_scr, act_scr = refs
    g = g_ref[...]
    nv = n // SUBLANES
    nslab = D_MODEL // LANES

    def put(r0, x):
        for l in range(nslab):
            nat_scr[l, r0:r0 + x.shape[0], :] = x[:, l * LANES:(l + 1) * LANES]
    keep = jnp.where(i < n_tiles - 1, 1.0, 0.0)
    put(0, _rmsnorm(hp, g))
    put(FFN_HALO, _rmsnorm(hc, g))
    put(FFN_HALO + tm, _rmsnorm(hn, g) * keep)

    def to_strided(jp, _):
        r = pl.ds(pl.multiple_of(jp * BF16_ROWS, BF16_ROWS), BF16_ROWS)
        for l in range(nslab):
            two = jnp.concatenate([nat_scr[l, pl.ds(2 * jp, SUBLANES, stride=nv), :],
                                   nat_scr[l, pl.ds(2 * jp + 1, SUBLANES, stride=nv), :]], axis=0)
            hx_scr[r, l * LANES:(l + 1) * LANES] = two.astype(BF16)
        return 0
    lax.fori_loop(0, nv // 2, to_strided, 0)

    def conv(u, cols):
        prev = jnp.concatenate([pltpu.roll(u[n - SUBLANES:], 1, 0), u[:n - SUBLANES]], axis=0)
        nxt = jnp.concatenate([u[SUBLANES:], pltpu.roll(u[:SUBLANES], SUBLANES - 1, 0)], axis=0)
        return (cw_ref[0:1, cols] * prev + cw_ref[1:2, cols] * u + cw_ref[2:3, cols] * nxt) + cb_ref[0:1, cols]

    for c in range(D_FF // FFN_TF):
        ca = slice(c * FFN_TF, (c + 1) * FFN_TF)
        cg = slice(D_FF + c * FFN_TF, D_FF + (c + 1) * FFN_TF)
        ua = jnp.dot(hx_scr[...], wup_ref[:, ca], preferred_element_type=F32)
        ug = jnp.dot(hx_scr[...], wup_ref[:, cg], preferred_element_type=F32)
        act_scr[:, ca] = (_gelu(conv(ug, cg)) * conv(ua, ca)).astype(BF16)

    ys = jnp.dot(act_scr[...], wdn_ref[...], preferred_element_type=F32)
    for j in range(nv):
        for l in range(nslab):
            nat_scr[l, pl.ds(j, SUBLANES, stride=nv), :] = ys[j * SUBLANES:(j + 1) * SUBLANES,
                                                              l * LANES:(l + 1) * LANES]
    y = hc + jnp.concatenate([nat_scr[l, FFN_HALO:FFN_HALO + tm, :] for l in range(nslab)], axis=1)
    if final:
        o_ref[...] = _rmsnorm(y, gf_ref[...])
    else:
        o_ref[...] = jnp.where(_valid_rows(i, tm, lp, y.shape), y, 0.0)


def _conv_ffn(h2, g, w_up, conv_w, conv_b, w_down, g_final, bsz, lp, final, o=None, w_o=None):
    rows, d = h2.shape
    const = dict(pipeline_mode=pl.Buffered(1))
    cw = jnp.pad(conv_w.astype(F32), ((0, SUBLANES - CONV_F), (0, 0)))
    if final:
        seq = lp - FRONT
        tm = _row_tile(seq, seq, mult=BLOCK)
        n_tiles, n_cur = seq // tm, tm // BLOCK
        grid = (bsz, n_tiles)
        last = lp // FFN_HALO - 1
        h_in = h2.reshape(bsz, lp, d)
        prev = pl.BlockSpec((None, FFN_HALO, d), lambda b, j: (b, (FRONT + j * tm) // FFN_HALO - 1, 0))
        cur = [pl.BlockSpec((None, BLOCK, d), lambda b, j, k=k: (b, 1 + j * n_cur + k, 0)) for k in range(n_cur)]
        nxt = pl.BlockSpec((None, FFN_HALO, d),
                           lambda b, j: (b, jnp.minimum((FRONT + (j + 1) * tm) // FFN_HALO, last), 0))
        out_spec = pl.BlockSpec((None, tm, d), lambda b, j: (b, j, 0))
        out_shape = jax.ShapeDtypeStruct((bsz, seq, d), F32)

        def whole(shape):
            return pl.BlockSpec(shape, lambda b, j: (0,) * len(shape))
        extra_specs = [prev] + cur + [nxt, pl.BlockSpec(w_o.shape, whole(w_o.shape).index_map, **const)]
        extra_ops = [o.reshape(bsz, lp, d)] * (2 + n_cur) + [w_o.astype(BF16)]
        extra_scratch = [pltpu.VMEM((tm + 2 * FFN_HALO, d), BF16)]
    else:
        tm = _row_tile(rows, lp, cap=FFN_TM)
        n_tiles, n_cur = rows // tm, 1
        grid = (n_tiles,)
        hb = tm // FFN_HALO
        last = rows // FFN_HALO - 1
        h_in = h2
        prev = pl.BlockSpec((FFN_HALO, d), lambda i: (jnp.maximum(i * hb - 1, 0), 0))
        cur = [pl.BlockSpec((tm, d), lambda i: (i, 0))]
        nxt = pl.BlockSpec((FFN_HALO, d), lambda i: (jnp.minimum((i + 1) * hb, last), 0))
        out_spec = pl.BlockSpec((tm, d), lambda i: (i, 0))
        out_shape = jax.ShapeDtypeStruct((rows, d), F32)

        def whole(shape):
            return pl.BlockSpec(shape, lambda i: (0,) * len(shape))
        extra_specs, extra_ops, extra_scratch = [], [], []
    weights = [pl.BlockSpec((1, d), whole((1, d)).index_map),
               pl.BlockSpec((d, 2 * D_FF), whole((d, 2 * D_FF)).index_map, **const),
               whole((SUBLANES, 2 * D_FF)), whole((1, 2 * D_FF)),
               pl.BlockSpec((D_FF, d), whole((D_FF, d)).index_map, **const),
               whole((1, d))]
    return pl.pallas_call(
        functools.partial(_ffn_kernel, lp=lp, tm=tm, n_tiles=n_tiles, final=final, n_cur=n_cur),
        grid=grid,
        in_specs=[prev] + cur + [nxt] + extra_specs + weights,
        out_specs=out_spec,
        out_shape=out_shape,
        scratch_shapes=[pltpu.VMEM((d // LANES, tm + 2 * FFN_HALO, LANES), F32),
                        pltpu.VMEM((tm + 2 * FFN_HALO, d), BF16),
                        pltpu.VMEM((tm + 2 * FFN_HALO, D_FF), BF16)] + extra_scratch,
        compiler_params=_cparams(len(grid)),
        name="conv_ffn",
    )(*([h_in] * (2 + n_cur)), *extra_ops, g.astype(F32).reshape(1, d), w_up.astype(BF16), cw,
      conv_b.astype(F32).reshape(1, 2 * D_FF), w_down.astype(BF16), g_final.astype(F32).reshape(1, d))


def _trunk(x, prm):
    bsz, seq, d = x.shape
    lp = seq + FRONT
    assert seq % BLOCK == 0 and d == D_MODEL

    h, u_a, x_b, g_b = _in_proj(x, prm['meta_tokens'], prm['norm_mix_g'][0], prm['w_in_ab'][0].astype(BF16), lp)
    s5w = _s5_weights(prm['s5_lambda_re'][0], prm['s5_lambda_im'][0], prm['s5_log_dt'][0],
                      prm['s5_b_re'][0], prm['s5_b_im'][0], prm['s5_c_re'][0], prm['s5_c_im'][0], prm['s5_d'][0])
    y_a = _s5_mixer(u_a, s5w, bsz, lp)
    lruw = _lru_weights(prm['lru_conv_w'][0], prm['lru_conv_b'][0], prm['lru_w_r'][0], prm['lru_b_r'][0],
                        prm['lru_w_i'][0], prm['lru_b_i'][0], prm['lru_lambda'][0])
    hf, hb = _lru_mixer(x_b, lruw, bsz, lp)
    h = _ab_out(y_a, hf, hb, g_b, h, prm['w_glu'][0], prm['b_glu'][0], prm['w_out_ab'][0], lp)
    h = _conv_ffn(h, prm['norm_ffn_g'][0], prm['w_up'][0], prm['ffn_conv_w'][0], prm['ffn_conv_b'][0],
                  prm['w_down'][0], prm['final_norm_g'], bsz, lp, final=False)

    dq, dkv = N_Q_HEADS * HEAD_DIM, N_KV_HEADS * HEAD_DIM
    head_order = jnp.asarray([h_ for pair in _pair_heads() for h_ in pair])
    col_order = (head_order[:, None] * HEAD_DIM + jnp.arange(HEAD_DIM)[None, :]).reshape(-1)
    w_qkv = prm['w_qkv'][0]
    w_qkv = jnp.concatenate([w_qkv[:, :dq][:, col_order], w_qkv[:, dq:]], axis=1).astype(BF16)
    q, k, v = _norm_mm(h, prm['norm_mix_g'][1], w_qkv, (dq, dkv, dkv),
                       (LOG2E / math.sqrt(HEAD_DIM), 1.0, 1.0), lp)
    o = _attention(q, k, v, prm['attn_sink'][0], bsz, lp)
    return _conv_ffn(h, prm['norm_ffn_g'][1], prm['w_up'][1], prm['ffn_conv_w'][1], prm['ffn_conv_b'][1],
                     prm['w_down'][1], prm['final_norm_g'], bsz, lp, final=True, o=o, w_o=prm['w_o'][0][col_order])


def kernel(x_prompt, x_sample, meta_tokens, norm_mix_g, norm_ffn_g, final_norm_g, w_in_ab, s5_lambda_re, s5_lambda_im, s5_log_dt, s5_b_re, s5_b_im, s5_c_re, s5_c_im, s5_d, w_glu, b_glu, lru_conv_w, lru_conv_b, lru_w_r, lru_b_r, lru_w_i, lru_b_i, lru_lambda, w_out_ab, w_qkv, w_o, attn_sink, w_up, ffn_conv_w, ffn_conv_b, w_down):
    prm = dict(meta_tokens=meta_tokens, norm_mix_g=norm_mix_g, norm_ffn_g=norm_ffn_g, final_norm_g=final_norm_g,
               w_in_ab=w_in_ab, s5_lambda_re=s5_lambda_re, s5_lambda_im=s5_lambda_im, s5_log_dt=s5_log_dt,
               s5_b_re=s5_b_re, s5_b_im=s5_b_im, s5_c_re=s5_c_re, s5_c_im=s5_c_im, s5_d=s5_d, w_glu=w_glu,
               b_glu=b_glu, lru_conv_w=lru_conv_w, lru_conv_b=lru_conv_b, lru_w_r=lru_w_r, lru_b_r=lru_b_r,
               lru_w_i=lru_w_i, lru_b_i=lru_b_i, lru_lambda=lru_lambda, w_out_ab=w_out_ab, w_qkv=w_qkv,
               w_o=w_o, attn_sink=attn_sink, w_up=w_up, ffn_conv_w=ffn_conv_w, ffn_conv_b=ffn_conv_b,
               w_down=w_down)
    return (_trunk(x_prompt, prm), _trunk(x_sample, prm))
```
